```python
import math
import jax
import jax.numpy as jnp
from jax import lax
import numpy as np

D_MODEL = 2048
BATCH = 4
SEQ = 2048
DEPTH = 1
DEC_BATCH = 128
DEC_SEQ = 8
PAST_LEN = 16384
PAGE_SIZE = 128

EPS = 1e-6
N_MEM = 256
N_BRANCH = 3
GLA_HEADS = 4
GLA_DK = D_MODEL // 16
GLA_DV = D_MODEL // 8
GLA_QK = GLA_HEADS * GLA_DK
GLA_V = GLA_HEADS * GLA_DV
GLA_RANK = 16
GLA_TAU = 16.0
GLA_CHUNK = 16
GDN_HEADS = 8
GDN_DK = D_MODEL // 16
GDN_DV = D_MODEL // 16
GDN_QK = GDN_HEADS * GDN_DK
GDN_V = GDN_HEADS * GDN_DV
GDN_CONV = 4
GDN_CONV_CH = 2 * GDN_QK + GDN_V
GDN_CHUNK = 64
MEM_HEADS = 4
MEM_DH = D_MODEL // 8
MEM_W = MEM_HEADS * MEM_DH
PEER_KEYS = 128
PEER_EXPERTS = PEER_KEYS * PEER_KEYS
PEER_HEADS = 8
PEER_DQ = 256
PEER_HALF = PEER_DQ // 2
PEER_TOPK = 16
PEER_BLOCK = 128
IN_SPLITS = (GLA_QK, GLA_QK, GLA_V, GLA_V, GLA_RANK, GDN_CONV_CH, GDN_V, GDN_HEADS, GDN_HEADS, MEM_W, N_BRANCH * D_MODEL)
IN_COLS = 2 * GLA_QK + 2 * GLA_V + GLA_RANK + GDN_CONV_CH + GDN_V + 2 * GDN_HEADS + MEM_W + N_BRANCH * D_MODEL

kernel_name = 'hybrid_gla_gdn_peer_decoder_step'


def rmsnorm(x, g):
    x32 = x.astype(jnp.float32)
    y = x32 * lax.rsqrt(jnp.mean(x32 * x32, axis=-1, keepdims=True) + EPS) * g.astype(jnp.float32)
    return y.astype(x.dtype)


def head_rmsnorm(o, g):
    return o * lax.rsqrt(jnp.mean(o * o, axis=-1, keepdims=True) + EPS) * g.astype(jnp.float32)


def l2norm(t):
    return t * lax.rsqrt(jnp.sum(t * t, axis=-1, keepdims=True) + EPS)


def _pad_seq(t, pad):
    return jnp.pad(t, [(0, 0), (0, pad)] + [(0, 0)] * (t.ndim - 2))


def _split_cols(t, sizes):
    out = []
    start = 0
    for n in sizes:
        out.append(t[..., start:start + n])
        start += n
    return out


def gla_chunked(q, k, v, g, s0):
    B, L, H, DK = q.shape
    DV = v.shape[-1]
    C = GLA_CHUNK
    pad = (-L) % C
    n = (L + pad) // C

    def chunks(t):
        t = _pad_seq(t, pad)
        return t.reshape((B, n, C) + t.shape[2:]).swapaxes(0, 1)

    incl = jnp.tril(jnp.ones((C, C), dtype=bool))

    def step(s, inp):
        qc, kc, vc, gc = inp
        b = jnp.cumsum(gc, axis=1)
        o_inter = jnp.einsum('bchk,bhkv->bchv', qc * jnp.exp(b), s)
        diff = b[:, :, None] - b[:, None, :]
        dec = jnp.exp(jnp.where(incl[None, :, :, None, None], diff, -jnp.inf))
        att = jnp.einsum('bihk,bjhk,bijhk->bhij', qc, kc, dec)
        o = o_inter + jnp.einsum('bhij,bjhv->bihv', att, vc)
        b_last = b[:, -1]
        k_dec = kc * jnp.exp(b_last[:, None] - b)
        s_new = jnp.exp(b_last)[..., None] * s + jnp.einsum('bchk,bchv->bhkv', k_dec, vc)
        return s_new, o

    s_fin, o = lax.scan(step, s0, (chunks(q), chunks(k), chunks(v), chunks(g)))
    o = o.swapaxes(0, 1).reshape(B, n * C, H, DV)[:, :L]
    return o, s_fin


def gdn_chunked(q, k, v, g, beta, s0):
    B, L, H, DK = q.shape
    DV = v.shape[-1]
    C = GDN_CHUNK
    pad = (-L) % C
    n = (L + pad) // C

    def chunks(t):
        t = _pad_seq(t, pad)
        return t.reshape((B, n, C) + t.shape[2:]).swapaxes(0, 1)

    incl = jnp.tril(jnp.ones((C, C), dtype=bool))
    strict = jnp.tril(jnp.ones((C, C), dtype=bool), k=-1)
    eye = jnp.eye(C, dtype=jnp.float32)

    def step(s, inp):
        qc, kc, vc, gc, bc = inp
        b = jnp.cumsum(gc, axis=1)
        diff = b[:, :, None] - b[:, None, :]
        gam = jnp.exp(jnp.where(incl[None, :, :, None], diff, -jnp.inf)).transpose(0, 3, 1, 2)
        kk = jnp.einsum('bihk,bjhk->bhij', kc, kc)
        lower = bc.transpose(0, 2, 1)[..., :, None] * jnp.where(strict, gam, 0.0) * kk
        m = eye + lower
        ks = jnp.einsum('bchk,bhkv->bchv', kc, s)
        rhs = (bc[..., None] * (vc - jnp.exp(b)[..., None] * ks)).transpose(0, 2, 1, 3)
        u = lax.linalg.triangular_solve(m, rhs, left_side=True, lower=True, unit_diagonal=True)
        qk = jnp.einsum('bihk,bjhk->bhij', qc, kc) * gam
        o = jnp.exp(b)[..., None] * jnp.einsum('bchk,bhkv->bchv', qc, s) + jnp.einsum('bhij,bhjv->bihv', qk, u)
        b_last = b[:, -1]
        w = jnp.exp(b_last[:, None] - b)
        s_new = jnp.exp(b_last)[..., None, None] * s + jnp.einsum('bchk,bhcv->bhkv', kc * w[..., None], u)
        return s_new, o

    s_fin, o = lax.scan(step, s0, (chunks(q), chunks(k), chunks(v), chunks(g), chunks(beta)))
    o = o.swapaxes(0, 1).reshape(B, n * C, H, DV)[:, :L]
    return o, s_fin


def memory_kv(mem, g, w_kv):
    B, M, _ = mem.shape
    kv = rmsnorm(mem, g) @ w_kv
    k = kv[..., :MEM_W].reshape(B, M, MEM_HEADS, MEM_DH)
    v = kv[..., MEM_W:].reshape(B, M, MEM_HEADS, MEM_DH)
    return k, v


def mixer(h, mem_k, mem_v, s_gla, s_gdn, conv_buf, lw):
    B, L, _ = h.shape
    f32 = jnp.float32
    dt = h.dtype
    proj = h @ lw['w_in']
    gq, gk, gv, gr, glr, dqkv, dz, da, db, mq, gates = _split_cols(proj, IN_SPLITS)

    q = gq.reshape(B, L, GLA_HEADS, GLA_DK).astype(f32) * (GLA_DK ** -0.5)
    k = gk.reshape(B, L, GLA_HEADS, GLA_DK).astype(f32)
    v = gv.reshape(B, L, GLA_HEADS, GLA_DV).astype(f32)
    logit = (glr @ lw['w_gla_gate'] + lw['b_gla_gate']).astype(f32)
    g = (jax.nn.log_sigmoid(logit) / GLA_TAU).reshape(B, L, GLA_HEADS, GLA_DK)
    o_a, s_gla_new = gla_chunked(q, k, v, g, s_gla.astype(f32))
    o_a = head_rmsnorm(o_a, lw['gla_norm_g']) * jax.nn.silu(gr.reshape(B, L, GLA_HEADS, GLA_DV).astype(f32))
    o_a = o_a.reshape(B, L, GLA_V).astype(dt)

    xpad = jnp.concatenate([conv_buf.astype(dt), dqkv], axis=1)
    conv = lax.conv_general_dilated(xpad, lw['gdn_conv_w'].astype(dt)[:, None, :], (1,), 'VALID',
                                    dimension_numbers=('NWC', 'WIO', 'NWC'), feature_group_count=GDN_CONV_CH)
    conv_new = xpad[:, -(GDN_CONV - 1):]
    act = jax.nn.silu(conv.astype(f32))
    cq, ck, cv = _split_cols(act, (GDN_QK, GDN_QK, GDN_V))
    q2 = l2norm(cq.reshape(B, L, GDN_HEADS, GDN_DK)) * (GDN_DK ** -0.5)
    k2 = l2norm(ck.reshape(B, L, GDN_HEADS, GDN_DK))
    v2 = cv.reshape(B, L, GDN_HEADS, GDN_DV)
    beta = jax.nn.sigmoid(db.astype(f32))
    gdec = -jnp.exp(lw['gdn_a_log'].astype(f32)) * jax.nn.softplus(da.astype(f32) + lw['gdn_dt_bias'].astype(f32))
    o_b, s_gdn_new = gdn_chunked(q2, k2, v2, gdec, beta, s_gdn.astype(f32))
    o_b = head_rmsnorm(o_b, lw['gdn_norm_g']) * jax.nn.silu(dz.reshape(B, L, GDN_HEADS, GDN_DV).astype(f32))
    o_b = o_b.reshape(B, L, GDN_V).astype(dt)

    qm = mq.reshape(B, L, MEM_HEADS, MEM_DH)
    scores = jnp.einsum('blhd,bmhd->bhlm', qm, mem_k.astype(dt)).astype(f32) * (MEM_DH ** -0.5)
    p = jax.nn.softmax(scores, axis=-1).astype(dt)
    o_c = jnp.einsum('bhlm,bmhd->blhd', p, mem_v.astype(dt)).reshape(B, L, MEM_W)

    gate = jax.nn.sigmoid((gates + lw['b_gates']).astype(f32)).astype(dt)
    g_a, g_b, g_c = _split_cols(gate, (D_MODEL, D_MODEL, D_MODEL))
    merged = g_a * (o_a @ lw['w_br_gla']) + g_b * (o_b @ lw['w_br_gdn']) + g_c * (o_c @ lw['w_br_mem'])
    out = merged @ lw['w_out']
    return out, s_gla_new.astype(dt), s_gdn_new.astype(dt), conv_new


def peer(h, w_q, sub_k1, sub_k2, u_tab, v_tab):
    T = h.shape[0]
    pad = (-T) % PEER_BLOCK
    blocks = jnp.pad(h, ((0, pad), (0, 0))).reshape(-1, PEER_BLOCK, D_MODEL)
    f32 = jnp.float32

    def one(hb):
        q = (hb @ w_q).astype(f32).reshape(PEER_BLOCK, PEER_HEADS, 2, PEER_HALF)
        s1 = jnp.einsum('thd,nd->thn', q[:, :, 0], sub_k1.astype(f32))
        s2 = jnp.einsum('thd,nd->thn', q[:, :, 1], sub_k2.astype(f32))
        t1, i1 = lax.top_k(s1, PEER_TOPK)
        t2, i2 = lax.top_k(s2, PEER_TOPK)
        cand = (t1[..., :, None] + t2[..., None, :]).reshape(PEER_BLOCK, PEER_HEADS, PEER_TOPK * PEER_TOPK)
        best, j = lax.top_k(cand, PEER_TOPK)
        ids = (jnp.take_along_axis(i1, j // PEER_TOPK, axis=-1) * PEER_KEYS
               + jnp.take_along_axis(i2, j % PEER_TOPK, axis=-1))
        w = jax.nn.softmax(best, axis=-1)
        u = jnp.take(u_tab, ids, axis=0)
        a = jax.nn.gelu(jnp.einsum('td,thkd->thk', hb, u).astype(f32), approximate=False)
        vv = jnp.take(v_tab, ids, axis=0)
        return jnp.einsum('thk,thkd->td', (w * a).astype(hb.dtype), vv)

    out = lax.map(one, blocks).reshape(-1, D_MODEL)[:T]
    return out.astype(h.dtype)


def layer(x, mem_k, mem_v, s_gla, s_gdn, conv_buf, lw):
    mix, s_gla, s_gdn, conv_buf = mixer(rmsnorm(x, lw['norm_mix_g']), mem_k, mem_v, s_gla, s_gdn, conv_buf, lw)
    x = x + mix
    h2 = rmsnorm(x, lw['norm_ffn_g'])
    ff = peer(h2.reshape(-1, D_MODEL), lw['peer_wq'], lw['peer_k1'], lw['peer_k2'], lw['peer_u'], lw['peer_v'])
    x = x + ff.reshape(x.shape)
    return x, s_gla, s_gdn, conv_buf


def setup_inputs(seed: int = 0) -> dict:
    key = jax.random.key(seed)
    ks = iter(jax.random.split(key, 40))
    nrm = lambda shape, scale: jax.random.normal(next(ks), shape, jnp.float32) * scale
    gain = lambda shape: 1.0 + 0.01 * jax.random.normal(next(ks), shape, jnp.float32)
    dt_min, dt_max = 0.001, 0.1
    dtv = jnp.exp(jax.random.uniform(next(ks), (DEPTH, GDN_HEADS), jnp.float32) * (math.log(dt_max) - math.log(dt_min)) + math.log(dt_min))
    inp = {
        'x_prompt': nrm((BATCH, SEQ, D_MODEL), 1.0),
        'x_sample': nrm((DEC_BATCH, DEC_SEQ, D_MODEL), 1.0),
        'mem_prompt': nrm((BATCH, N_MEM, D_MODEL), 1.0),
        'cache_mem_k': nrm((DEPTH, DEC_BATCH, N_MEM, MEM_HEADS, MEM_DH), 1.0),
        'cache_mem_v': nrm((DEPTH, DEC_BATCH, N_MEM, MEM_HEADS, MEM_DH), 1.0),
        'state_gla': nrm((DEPTH, DEC_BATCH, GLA_HEADS, GLA_DK, GLA_DV), 0.1),
        'state_gdn': nrm((DEPTH, DEC_BATCH, GDN_HEADS, GDN_DK, GDN_DV), 0.1),
        'state_conv': nrm((DEPTH, DEC_BATCH, GDN_CONV - 1, GDN_CONV_CH), 1.0),
        'norm_mix_g': gain((DEPTH, D_MODEL)),
        'norm_mem_g': gain((DEPTH, D_MODEL)),
        'w_in': nrm((DEPTH, D_MODEL, IN_COLS), D_MODEL ** -0.5),
        'w_gla_gate': nrm((DEPTH, GLA_RANK, GLA_QK), GLA_RANK ** -0.5),
        'b_gla_gate': nrm((DEPTH, GLA_QK), 0.01),
        'gla_norm_g': gain((DEPTH, GLA_DV)),
        'gdn_conv_w': nrm((DEPTH, GDN_CONV, GDN_CONV_CH), GDN_CONV ** -0.5),
        'gdn_a_log': jnp.log(jax.random.uniform(next(ks), (DEPTH, GDN_HEADS), jnp.float32, 1.0, 16.0)),
        'gdn_dt_bias': dtv + jnp.log(-jnp.expm1(-dtv)),
        'gdn_norm_g': gain((DEPTH, GDN_DV)),
        'w_mem_kv': nrm((DEPTH, D_MODEL, 2 * MEM_W), D_MODEL ** -0.5),
        'w_br_gla': nrm((DEPTH, GLA_V, D_MODEL), GLA_V ** -0.5),
        'w_br_gdn': nrm((DEPTH, GDN_V, D_MODEL), GDN_V ** -0.5),
        'w_br_mem': nrm((DEPTH, MEM_W, D_MODEL), MEM_W ** -0.5),
        'b_gates': nrm((DEPTH, N_BRANCH * D_MODEL), 0.01),
        'w_out': nrm((DEPTH, D_MODEL, D_MODEL), D_MODEL ** -0.5),
        'norm_ffn_g': gain((DEPTH, D_MODEL)),
        'peer_wq': nrm((DEPTH, D_MODEL, PEER_HEADS * PEER_DQ), D_MODEL ** -0.5),
        'peer_k1': nrm((DEPTH, PEER_KEYS, PEER_HALF), PEER_HALF ** -0.5),
        'peer_k2': nrm((DEPTH, PEER_KEYS, PEER_HALF), PEER_HALF ** -0.5),
        'peer_u': nrm((DEPTH, PEER_EXPERTS, D_MODEL), D_MODEL ** -0.5),
        'peer_v': nrm((DEPTH, PEER_EXPERTS, D_MODEL), PEER_HEADS ** -0.5),
        'final_norm_g': gain((D_MODEL,)),
    }
    return inp


def reference(x_prompt, x_sample, mem_prompt, cache_mem_k, cache_mem_v, state_gla, state_gdn, state_conv,
              norm_mix_g, norm_mem_g, w_in, w_gla_gate, b_gla_gate, gla_norm_g, gdn_conv_w, gdn_a_log,
              gdn_dt_bias, gdn_norm_g, w_mem_kv, w_br_gla, w_br_gdn, w_br_mem, b_gates, w_out, norm_ffn_g,
              peer_wq, peer_k1, peer_k2, peer_u, peer_v, final_norm_g):
    xp = x_prompt
    xs = x_sample
    dt = x_prompt.dtype
    gla_p, gdn_p, conv_p, mk_p, mv_p = [], [], [], [], []
    gla_s, gdn_s, conv_s = [], [], []
    for l in range(DEPTH):
        lw = {
            'norm_mix_g': norm_mix_g[l], 'w_in': w_in[l], 'w_gla_gate': w_gla_gate[l], 'b_gla_gate': b_gla_gate[l],
            'gla_norm_g': gla_norm_g[l], 'gdn_conv_w': gdn_conv_w[l], 'gdn_a_log': gdn_a_log[l],
            'gdn_dt_bias': gdn_dt_bias[l], 'gdn_norm_g': gdn_norm_g[l], 'w_br_gla': w_br_gla[l],
            'w_br_gdn': w_br_gdn[l], 'w_br_mem': w_br_mem[l], 'b_gates': b_gates[l], 'w_out': w_out[l],
            'norm_ffn_g': norm_ffn_g[l], 'peer_wq': peer_wq[l], 'peer_k1': peer_k1[l], 'peer_k2': peer_k2[l],
            'peer_u': peer_u[l], 'peer_v': peer_v[l],
        }
        mk, mv = memory_kv(mem_prompt, norm_mem_g[l], w_mem_kv[l])
        z_gla = jnp.zeros((BATCH, GLA_HEADS, GLA_DK, GLA_DV), dt)
        z_gdn = jnp.zeros((BATCH, GDN_HEADS, GDN_DK, GDN_DV), dt)
        z_conv = jnp.zeros((BATCH, GDN_CONV - 1, GDN_CONV_CH), dt)
        xp, a1, a2, a3 = layer(xp, mk, mv, z_gla, z_gdn, z_conv, lw)
        gla_p.append(a1)
        gdn_p.append(a2)
        conv_p.append(a3)
        mk_p.append(mk)
        mv_p.append(mv)
        xs, b1, b2, b3 = layer(xs, cache_mem_k[l], cache_mem_v[l], state_gla[l], state_gdn[l], state_conv[l], lw)
        gla_s.append(b1)
        gdn_s.append(b2)
        conv_s.append(b3)
    y_prompt = rmsnorm(xp, final_norm_g)
    y_sample = rmsnorm(xs, final_norm_g)
    return (y_prompt, y_sample, jnp.stack(gla_p), jnp.stack(gdn_p), jnp.stack(conv_p), jnp.stack(mk_p), jnp.stack(mv_p),
            jnp.stack(gla_s), jnp.stack(gdn_s), jnp.stack(conv_s))
```

```python
import functools

import jax
import jax.numpy as jnp
from jax import lax
from jax.experimental import pallas as pl
from jax.experimental.pallas import tpu as pltpu

F32 = jnp.float32
BF16 = jnp.bfloat16
I32 = jnp.int32
HI = lax.Precision.HIGHEST
NEG_INF = float("-inf")

D_MODEL = 2048
EPS = 1e-6
N_MEM = 256
GLA_HEADS, GLA_DK, GLA_DV, GLA_RANK, GLA_TAU = 4, 128, 256, 16, 16.0
GLA_QK, GLA_V = GLA_HEADS * GLA_DK, GLA_HEADS * GLA_DV
GDN_HEADS, GDN_DK, GDN_DV, GDN_CONV = 8, 128, 128, 4
GDN_QK, GDN_V = GDN_HEADS * GDN_DK, GDN_HEADS * GDN_DV
GDN_CONV_CH = 2 * GDN_QK + GDN_V
MEM_HEADS, MEM_DH = 4, 256
MEM_W = MEM_HEADS * MEM_DH
PEER_KEYS, PEER_HEADS, PEER_HALF, PEER_TOPK = 128, 8, 128, 16
PEER_EXPERTS = PEER_KEYS * PEER_KEYS
N_GATES = 3 * D_MODEL
IN_SPLITS = (GLA_QK, GLA_QK, GLA_V, GLA_V, GLA_RANK, GDN_CONV_CH, GDN_V, GDN_HEADS, GDN_HEADS, MEM_W, N_GATES)
IN_NAMES = ("gq", "gk", "gv", "gr", "glr", "dqkv", "dz", "da", "db", "mq", "gates")
MAIN_NAMES = ("gq", "gk", "gv", "gr", "dqkv", "dz", "mq", "gates")
N_MAIN = 2 * GLA_QK + 2 * GLA_V + GDN_CONV_CH + GDN_V + MEM_W + N_GATES
SMALL_W = 128
DA_LANE, DB_LANE = GLA_RANK, GLA_RANK + GDN_HEADS

VMEM_LIMIT_BYTES = 56 * 1024 * 1024
SUBLANES, LANES = 8, 128


def _cparams(*sem):
    return pltpu.CompilerParams(dimension_semantics=sem, vmem_limit_bytes=VMEM_LIMIT_BYTES)


def _sds(shape, dtype=F32):
    return jax.ShapeDtypeStruct(shape, dtype)


def _mm(a, b, **kw):
    return jnp.dot(a, b, preferred_element_type=F32, **kw)


def _nt(a, b, **kw):
    return lax.dot_general(a, b, (((1,), (1,)), ((), ())), preferred_element_type=F32, **kw)


def _tn(a, b):
    return lax.dot_general(a, b, (((0,), (0,)), ((), ())), preferred_element_type=F32)


def _mm3(a, b):
    ah = a.astype(BF16).astype(F32)
    bh = b.astype(BF16).astype(F32)
    return _mm(ah, bh) + (_mm(ah, b - bh) + _mm(a - ah, bh))


def _softplus(x):
    return jnp.maximum(x, 0.0) + jnp.log1p(jnp.exp(-jnp.abs(x)))


def _silu(x):
    return x * jax.nn.sigmoid(x)


def _rms(x, g):
    return x * lax.rsqrt(jnp.mean(x * x, axis=-1, keepdims=True) + EPS) * g


def _inproj_body(x_ref, g_ref, w_ref, ws_ref, o_ref, os_ref, xn_ref):
    @pl.when(pl.program_id(1) == 0)
    def _():
        xn = _rms(x_ref[...], g_ref[...]).astype(BF16)
        xn_ref[...] = xn
        os_ref[...] = _mm(xn, ws_ref[...])

    o_ref[...] = _mm(xn_ref[...], w_ref[...])


def _inproj(x2d, g, w_main, w_small, tm, tn):
    t, k = x2d.shape
    n = w_main.shape[1]
    return pl.pallas_call(
        _inproj_body,
        grid=(t // tm, n // tn),
        in_specs=[pl.BlockSpec((tm, k), lambda i, j: (i, 0)),
                  pl.BlockSpec((1, k), lambda i, j: (0, 0)),
                  pl.BlockSpec((k, tn), lambda i, j: (0, j)),
                  pl.BlockSpec((k, SMALL_W), lambda i, j: (0, 0))],
        out_specs=[pl.BlockSpec((tm, tn), lambda i, j: (i, j)),
                   pl.BlockSpec((tm, SMALL_W), lambda i, j: (i, 0))],
        out_shape=[_sds((t, n)), _sds((t, SMALL_W))],
        scratch_shapes=[pltpu.VMEM((tm, k), BF16)],
        compiler_params=_cparams("parallel", "arbitrary"),
        name="inproj",
    )(x2d, g, w_main, w_small)


def _normmm_body(x_ref, g_ref, w_ref, o_ref, xn_ref):
    @pl.when(pl.program_id(1) == 0)
    def _():
        xn_ref[...] = _rms(x_ref[...], g_ref[...]).astype(BF16)

    o_ref[...] = _mm(xn_ref[...], w_ref[...])


def _normmm(x2d, g, w, tm, tn):
    t, k = x2d.shape
    n = w.shape[1]
    return pl.pallas_call(
        _normmm_body,
        grid=(t // tm, n // tn),
        in_specs=[pl.BlockSpec((tm, k), lambda i, j: (i, 0)),
                  pl.BlockSpec((1, k), lambda i, j: (0, 0)),
                  pl.BlockSpec((k, tn), lambda i, j: (0, j))],
        out_specs=pl.BlockSpec((tm, tn), lambda i, j: (i, j)),
        out_shape=_sds((t, n)),
        scratch_shapes=[pltpu.VMEM((tm, k), BF16)],
        compiler_params=_cparams("parallel", "arbitrary"),
        name="memkv",
    )(x2d, g, w)


def _gla_body(*refs, ch, nc, has_state):
    if has_state:
        (q_ref, k_ref, v_ref, r_ref, sm_ref, wg_ref, bg_ref, ng_ref, tri_ref, s0_ref,
         o_ref, so_ref, st_scr, b_scr) = refs
    else:
        (q_ref, k_ref, v_ref, r_ref, sm_ref, wg_ref, bg_ref, ng_ref, tri_ref,
         o_ref, so_ref, st_scr, b_scr) = refs
    step = pl.program_id(1)

    @pl.when(step == 0)
    def _():
        for h in range(GLA_HEADS):
            if has_state:
                st_scr[h] = s0_ref[0, h].T
            else:
                st_scr[h] = jnp.zeros((GLA_DV, GLA_DK), F32)

    logit = _mm(sm_ref[:, 0:GLA_RANK], wg_ref[...], precision=HI) + bg_ref[...]
    g = (jnp.minimum(logit, 0.0) - jnp.log1p(jnp.exp(-jnp.abs(logit)))) * (1.0 / GLA_TAU)
    b_scr[...] = _mm(tri_ref[...], g, precision=HI)

    row_io = lax.broadcasted_iota(I32, (ch, GLA_DK), 0)
    lane_io = lax.broadcasted_iota(I32, (ch, ch), 1)

    def chunk(c, carry):
        rows = pl.ds(pl.multiple_of(c * ch, ch), ch)
        for h in range(GLA_HEADS):
            ksl = slice(h * GLA_DK, (h + 1) * GLA_DK)
            vsl = slice(h * GLA_DV, (h + 1) * GLA_DV)
            qh = q_ref[rows, ksl] * (GLA_DK ** -0.5)
            kh = k_ref[rows, ksl]
            vh = v_ref[rows, vsl]
            bh = b_scr[rows, ksl]
            st = st_scr[h]
            o = _nt(qh * jnp.exp(bh), st)
            att_t = jnp.zeros((ch, ch), F32)
            for i in range(ch):
                dec = jnp.exp(jnp.where(row_io <= i, bh[i:i + 1, :] - bh, NEG_INF))
                col = jnp.sum(kh * dec * qh[i:i + 1, :], axis=1, keepdims=True)
                att_t = jnp.where(lane_io == i, col, att_t)
            o = o + _tn(att_t, vh)
            b_last = bh[ch - 1:ch, :]
            st_scr[h] = jnp.exp(b_last) * st + _tn(vh, kh * jnp.exp(b_last - bh))
            rh = r_ref[rows, vsl]
            o_ref[rows, vsl] = _rms(o, ng_ref[...]) * _silu(rh)
        return carry

    lax.fori_loop(0, nc, chunk, 0)

    @pl.when(step == pl.num_programs(1) - 1)
    def _():
        for h in range(GLA_HEADS):
            so_ref[0, h] = st_scr[h].T


def _gla(proj, small, wts, s0, nb, seq, ch, lt):
    nsteps = seq // lt
    nc = lt // ch
    t = nb * seq
    rix = lambda b, s: b * nsteps + s
    idx = lax.broadcasted_iota(I32, (lt, lt), 0)
    jdx = lax.broadcasted_iota(I32, (lt, lt), 1)
    tri = ((jdx <= idx) & (idx // ch == jdx // ch)).astype(F32)
    in_specs = [pl.BlockSpec((lt, GLA_QK), lambda b, s: (rix(b, s), 0)),
                pl.BlockSpec((lt, GLA_QK), lambda b, s: (rix(b, s), 1)),
                pl.BlockSpec((lt, GLA_V), lambda b, s: (rix(b, s), 1)),
                pl.BlockSpec((lt, GLA_V), lambda b, s: (rix(b, s), 2)),
                pl.BlockSpec((lt, SMALL_W), lambda b, s: (rix(b, s), 0)),
                pl.BlockSpec((GLA_RANK, GLA_QK), lambda b, s: (0, 0)),
                pl.BlockSpec((1, GLA_QK), lambda b, s: (0, 0)),
                pl.BlockSpec((1, GLA_DV), lambda b, s: (0, 0)),
                pl.BlockSpec((lt, lt), lambda b, s: (0, 0))]
    args = [proj, proj, proj, proj, small, wts["w_gla_gate"], wts["b_gla_gate"], wts["gla_norm_g"], tri]
    if s0 is not None:
        in_specs.append(pl.BlockSpec((1, GLA_HEADS, GLA_DK, GLA_DV), lambda b, s: (b, 0, 0, 0)))
        args.append(s0)
    return pl.pallas_call(
        functools.partial(_gla_body, ch=ch, nc=nc, has_state=s0 is not None),
        grid=(nb, nsteps),
        in_specs=in_specs,
        out_specs=[pl.BlockSpec((lt, GLA_V), lambda b, s: (rix(b, s), 0)),
                   pl.BlockSpec((1, GLA_HEADS, GLA_DK, GLA_DV), lambda b, s: (b, 0, 0, 0))],
        out_shape=[_sds((t, GLA_V)), _sds((nb, GLA_HEADS, GLA_DK, GLA_DV))],
        scratch_shapes=[pltpu.VMEM((GLA_HEADS, GLA_DV, GLA_DK), F32), pltpu.VMEM((lt, GLA_QK), F32)],
        compiler_params=_cparams("parallel", "arbitrary"),
        name="gla",
    )(*args)


def _gdn_body(*refs, c, has_state, nsq):
    if has_state:
        (x_ref, z_ref, sm_ref, cw_ref, al_ref, dtb_ref, ng_ref, sel_ref, s0_ref, c0_ref,
         o_ref, so_ref, co_ref, st_scr, xp_scr) = refs
    else:
        (x_ref, z_ref, sm_ref, cw_ref, al_ref, dtb_ref, ng_ref, sel_ref,
         o_ref, so_ref, co_ref, st_scr, xp_scr) = refs
    step = pl.program_id(1)
    hist = GDN_CONV - 1
    base = SUBLANES - hist

    @pl.when(step == 0)
    def _():
        if has_state:
            st_scr[...] = s0_ref[0]
            xp_scr[base:SUBLANES, :] = c0_ref[0]
        else:
            st_scr[...] = jnp.zeros(st_scr.shape, F32)
            xp_scr[base:SUBLANES, :] = jnp.zeros((hist, GDN_CONV_CH), F32)

    xp_scr[SUBLANES:SUBLANES + c, :] = x_ref[...]
    conv = xp_scr[pl.ds(base, c), :] * cw_ref[0:1, :]
    for w in range(1, GDN_CONV):
        conv = conv + xp_scr[pl.ds(base + w, c), :] * cw_ref[w:w + 1, :]
    tail = xp_scr[pl.ds(c + base, hist), :]
    xp_scr[base:SUBLANES, :] = tail
    co_ref[0] = tail
    act = _silu(conv)

    sm = sm_ref[...]
    gdec = -jnp.exp(al_ref[...]) * _softplus(sm + dtb_ref[...])
    beta = jax.nn.sigmoid(sm)
    ii = lax.broadcasted_iota(I32, (c, c), 0)
    jj = lax.broadcasted_iota(I32, (c, c), 1)
    bcol = _mm((ii >= jj).astype(F32), gdec, precision=HI)
    brow = _nt(sel_ref[...], bcol, precision=HI)
    eye = (ii == jj).astype(F32)

    for h in range(GDN_HEADS):
        sl = slice(h * GDN_DK, (h + 1) * GDN_DK)
        cq = act[:, h * GDN_DK:(h + 1) * GDN_DK]
        ck = act[:, GDN_QK + h * GDN_DK:GDN_QK + (h + 1) * GDN_DK]
        vh = act[:, 2 * GDN_QK + h * GDN_DV:2 * GDN_QK + (h + 1) * GDN_DV]
        qh = cq * lax.rsqrt(jnp.sum(cq * cq, axis=-1, keepdims=True) + EPS) * (GDN_DK ** -0.5)
        kh = ck * lax.rsqrt(jnp.sum(ck * ck, axis=-1, keepdims=True) + EPS)
        bc = bcol[:, DA_LANE + h:DA_LANE + h + 1]
        br = brow[h:h + 1, :]
        bt = beta[:, DB_LANE + h:DB_LANE + h + 1]
        gam = jnp.exp(jnp.where(ii >= jj, bc - br, NEG_INF))
        kk = _nt(kh, kh)
        qk = _nt(qh, kh)
        p = -(bt * jnp.where(ii > jj, gam, 0.0) * kk)
        ainv = eye + p
        pk = p
        for _ in range(nsq):
            pk = _mm3(pk, pk)
            ainv = ainv + _mm3(ainv, pk)
        s_h = st_scr[h]
        ks = _mm(kh, s_h)
        qs = _mm(qh, s_h)
        eb = jnp.exp(bc)
        u = _mm3(ainv, bt * (vh - eb * ks))
        o = eb * qs + _mm(qk * gam, u)
        b_last = bc[c - 1:c, :]
        st_scr[h] = jnp.exp(b_last) * s_h + _tn(kh * jnp.exp(b_last - bc), u)
        o_ref[:, sl] = _rms(o, ng_ref[...]) * _silu(z_ref[:, sl])

    @pl.when(step == pl.num_programs(1) - 1)
    def _():
        so_ref[0] = st_scr[...]


def _gdn(proj, small, wts, s0, conv0, nb, seq, c):
    nsteps = seq // c
    t = nb * seq
    rix = lambda b, s: b * nsteps + s
    nsq = max(c.bit_length() - 2, 0)
    in_specs = [pl.BlockSpec((c, GDN_CONV_CH), lambda b, s: (rix(b, s), 1)),
                pl.BlockSpec((c, GDN_V), lambda b, s: (rix(b, s), 6)),
                pl.BlockSpec((c, SMALL_W), lambda b, s: (rix(b, s), 0)),
                pl.BlockSpec((GDN_CONV, GDN_CONV_CH), lambda b, s: (0, 0)),
                pl.BlockSpec((1, SMALL_W), lambda b, s: (0, 0)),
                pl.BlockSpec((1, SMALL_W), lambda b, s: (0, 0)),
                pl.BlockSpec((1, GDN_DV), lambda b, s: (0, 0)),
                pl.BlockSpec((GDN_HEADS, SMALL_W), lambda b, s: (0, 0))]
    args = [proj, proj, small, wts["gdn_conv_w"], wts["a_log_lanes"], wts["dt_bias_lanes"], wts["gdn_norm_g"],
            wts["head_sel"]]
    if s0 is not None:
        in_specs += [pl.BlockSpec((1, GDN_HEADS, GDN_DK, GDN_DV), lambda b, s: (b, 0, 0, 0)),
                     pl.BlockSpec((1, GDN_CONV - 1, GDN_CONV_CH), lambda b, s: (b, 0, 0))]
        args += [s0, conv0]
    return pl.pallas_call(
        functools.partial(_gdn_body, c=c, has_state=s0 is not None, nsq=nsq),
        grid=(nb, nsteps),
        in_specs=in_specs,
        out_specs=[pl.BlockSpec((c, GDN_V), lambda b, s: (rix(b, s), 0)),
                   pl.BlockSpec((1, GDN_HEADS, GDN_DK, GDN_DV), lambda b, s: (b, 0, 0, 0)),
                   pl.BlockSpec((1, GDN_CONV - 1, GDN_CONV_CH), lambda b, s: (b, 0, 0))],
        out_shape=[_sds((t, GDN_V)), _sds((nb, GDN_HEADS, GDN_DK, GDN_DV)), _sds((nb, GDN_CONV - 1, GDN_CONV_CH))],
        scratch_shapes=[pltpu.VMEM((GDN_HEADS, GDN_DK, GDN_DV), F32),
                        pltpu.VMEM((c + SUBLANES, GDN_CONV_CH), F32)],
        compiler_params=_cparams("parallel", "arbitrary"),
        name="gdn",
    )(*args)


def _xattn_body(q_ref, k_ref, v_ref, o_ref):
    for h in range(MEM_HEADS):
        sl = slice(h * MEM_DH, (h + 1) * MEM_DH)
        sc = _nt(q_ref[:, sl], k_ref[0, :, sl]) * (MEM_DH ** -0.5)
        p = jnp.exp(sc - jnp.max(sc, axis=-1, keepdims=True))
        p = p / jnp.sum(p, axis=-1, keepdims=True)
        o_ref[:, sl] = _mm(p, v_ref[0, :, sl])


def _xattn(proj, k_arr, k_cb, v_arr, v_cb, nb, seq, lt):
    nsteps = seq // lt
    t = nb * seq
    return pl.pallas_call(
        _xattn_body,
        grid=(nb, nsteps),
        in_specs=[pl.BlockSpec((lt, MEM_W), lambda b, s: (b * nsteps + s, 7)),
                  pl.BlockSpec((1, N_MEM, MEM_W), lambda b, s: (b, 0, k_cb)),
                  pl.BlockSpec((1, N_MEM, MEM_W), lambda b, s: (b, 0, v_cb))],
        out_specs=pl.BlockSpec((lt, MEM_W), lambda b, s: (b * nsteps + s, 0)),
        out_shape=_sds((t, MEM_W)),
        compiler_params=_cparams("parallel", "arbitrary"),
        name="xattn",
    )(proj, k_arr, v_arr)


def _merge_body(oa_ref, ob_ref, oc_ref, ga_ref, gb_ref, gc_ref, ba_ref, bb_ref, bc_ref,
                wa_ref, wb_ref, wc_ref, m_ref):
    acc = jax.nn.sigmoid(ga_ref[...] + ba_ref[...]) * _mm(oa_ref[...].astype(BF16), wa_ref[...])
    acc = acc + jax.nn.sigmoid(gb_ref[...] + bb_ref[...]) * _mm(ob_ref[...].astype(BF16), wb_ref[...])
    acc = acc + jax.nn.sigmoid(gc_ref[...] + bc_ref[...]) * _mm(oc_ref[...].astype(BF16), wc_ref[...])
    m_ref[...] = acc.astype(BF16)


def _merge(o_a, o_b, o_c, proj, wts, tm):
    t = o_a.shape[0]
    row = lambda w: pl.BlockSpec((tm, w), lambda i: (i, 0))
    gate = lambda cb: pl.BlockSpec((tm, D_MODEL), lambda i: (i, cb))
    bias = lambda cb: pl.BlockSpec((1, D_MODEL), lambda i: (0, cb))
    wspec = lambda k: pl.BlockSpec((k, D_MODEL), lambda i: (0, 0))
    return pl.pallas_call(
        _merge_body,
        grid=(t // tm,),
        in_specs=[row(GLA_V), row(GDN_V), row(MEM_W), gate(4), gate(5), gate(6), bias(0), bias(1), bias(2),
                  wspec(GLA_V), wspec(GDN_V), wspec(MEM_W)],
        out_specs=pl.BlockSpec((tm, D_MODEL), lambda i: (i, 0)),
        out_shape=_sds((t, D_MODEL), BF16),
        compiler_params=_cparams("parallel"),
        name="merge",
    )(o_a, o_b, o_c, proj, proj, proj, wts["b_gates"], wts["b_gates"], wts["b_gates"],
      wts["w_br_gla"], wts["w_br_gdn"], wts["w_br_mem"])


def _outproj_body(m_ref, w_ref, x_ref, g_ref, x1_ref, h2_ref):
    x1 = x_ref[...] + _mm(m_ref[...], w_ref[...])
    x1_ref[...] = x1
    h2_ref[...] = _rms(x1, g_ref[...]).astype(BF16)


def _outproj(merged, w_out, x2d, g, tm):
    t = x2d.shape[0]
    blk = lambda: pl.BlockSpec((tm, D_MODEL), lambda i: (i, 0))
    return pl.pallas_call(
        _outproj_body,
        grid=(t // tm,),
        in_specs=[blk(), pl.BlockSpec((D_MODEL, D_MODEL), lambda i: (0, 0)), blk(),
                  pl.BlockSpec((1, D_MODEL), lambda i: (0, 0))],
        out_specs=[blk(), blk()],
        out_shape=[_sds((t, D_MODEL)), _sds((t, D_MODEL), BF16)],
        compiler_params=_cparams("parallel"),
        name="outproj",
    )(merged, w_out, x2d, g)


def _peerq_body(h_ref, w_ref, q_ref):
    q_ref[...] = _mm(h_ref[...], w_ref[...]).astype(BF16)


def _peerq(h2, wq, tm):
    t = h2.shape[0]
    n = wq.shape[1]
    return pl.pallas_call(
        _peerq_body,
        grid=(t // tm,),
        in_specs=[pl.BlockSpec((tm, D_MODEL), lambda i: (i, 0)), pl.BlockSpec((D_MODEL, n), lambda i: (0, 0))],
        out_specs=pl.BlockSpec((tm, n), lambda i: (i, 0)),
        out_shape=_sds((t, n), BF16),
        compiler_params=_cparams("parallel"),
        name="peerq",
    )(h2, wq)


def _top16(x, tv_ref, slot):
    io = lax.broadcasted_iota(I32, x.shape, 0)
    rank = jnp.full(x.shape, PEER_TOPK, I32)
    for r in range(PEER_TOPK):
        m = jnp.max(x, axis=0, keepdims=True)
        idx = jnp.min(jnp.where(x == m, io, x.shape[0]), axis=0, keepdims=True)
        hit = io == idx
        rank = jnp.where(hit, r, rank)
        x = jnp.where(hit, NEG_INF, x)
        tv_ref[slot, r:r + 1, :] = m
    return rank


def _peer_select(h, qp_ref, k1_ref, k2_ref, tv_scr, lh_scr, p1_scr, r2_scr, p2_scr, tl):
    base = h * 2 * PEER_HALF
    s1 = _nt(k1_ref[...], qp_ref[:, base:base + PEER_HALF])
    s2 = _nt(k2_ref[...], qp_ref[:, base + PEER_HALF:base + 2 * PEER_HALF])
    rank1 = _top16(s1, tv_scr, 0)
    rank2 = _top16(s2, tv_scr, 1)
    t1 = tv_scr[0]
    t2 = tv_scr[1]
    cand = jnp.concatenate([t1[a:a + 1, :] + t2 for a in range(PEER_TOPK)], axis=0)
    code = lax.broadcasted_iota(I32, cand.shape, 0)
    a_io = lax.broadcasted_iota(I32, (PEER_TOPK, tl), 0)
    limit = jnp.zeros((PEER_TOPK, tl), I32)
    for _ in range(PEER_TOPK):
        m = jnp.max(cand, axis=0, keepdims=True)
        idx = jnp.min(jnp.where(cand == m, code, cand.shape[0]), axis=0, keepdims=True)
        cand = jnp.where(code == idx, NEG_INF, cand)
        limit = limit + (a_io == (idx >> 4)).astype(I32)
    p1s = jnp.exp(t1 - t1[0:1, :])
    p2s = jnp.exp(t2 - t2[0:1, :])
    inner = jnp.zeros((PEER_TOPK, tl), F32)
    for b in range(PEER_TOPK):
        inner = inner + jnp.where(limit > b, p2s[b:b + 1, :], 0.0)
    inv_z = 1.0 / jnp.sum(p1s * inner, axis=0, keepdims=True)
    lim_e1 = jnp.zeros(rank1.shape, I32)
    for a in range(PEER_TOPK):
        lim_e1 = jnp.where(rank1 == a, limit[a:a + 1, :], lim_e1)
    lh_scr[h] = lim_e1.astype(F32)
    p1_scr[h] = jnp.where(rank1 < PEER_TOPK, jnp.exp(s1 - t1[0:1, :]), 0.0) * inv_z
    r2_scr[h] = rank2.astype(F32)
    p2_scr[h] = jnp.exp(jnp.minimum(s2 - t2[0:1, :], 0.0))


def _peer_body(qp_ref, h2_ref, x1_ref, k1_ref, k2_ref, u_ref, vt_ref, fg_ref, y_ref,
               lh_scr, p1_scr, r2_scr, p2_scr, tv_scr, acc_scr, *, tl, te):
    e = pl.program_id(1)

    @pl.when(e == 0)
    def _():
        acc_scr[...] = jnp.zeros(acc_scr.shape, F32)
        for h in range(PEER_HEADS):
            _peer_select(h, qp_ref, k1_ref, k2_ref, tv_scr, lh_scr, p1_scr, r2_scr, p2_scr, tl)

    sc = _nt(u_ref[...], h2_ref[...])
    act = 0.5 * sc * (1.0 + lax.erf(sc * 0.7071067811865476))
    n_e1 = te // PEER_KEYS
    parts = []
    for i1 in range(n_e1):
        e1 = e * n_e1 + i1
        coef = jnp.zeros((PEER_KEYS, tl), F32)
        for h in range(PEER_HEADS):
            lim = lh_scr[h, pl.ds(e1, 1), :]
            p1 = p1_scr[h, pl.ds(e1, 1), :]
            coef = coef + jnp.where(r2_scr[h] < lim, p2_scr[h], 0.0) * p1
        parts.append((coef * act[i1 * PEER_KEYS:(i1 + 1) * PEER_KEYS, :]).astype(BF16))
    z = jnp.concatenate(parts, axis=0) if n_e1 > 1 else parts[0]
    acc_scr[...] += _mm(vt_ref[...], z)

    @pl.when(e == pl.num_programs(1) - 1)
    def _():
        y_ref[...] = _rms(x1_ref[...] + acc_scr[...].T, fg_ref[...])


def _peer(qp, h2, x1, wts, tl, te):
    t = h2.shape[0]
    tok = lambda: pl.BlockSpec((tl, D_MODEL), lambda i, e: (i, 0))
    keys = lambda: pl.BlockSpec((PEER_KEYS, PEER_HALF), lambda i, e: (0, 0))
    sel = lambda: pltpu.VMEM((PEER_HEADS, PEER_KEYS, tl), F32)
    return pl.pallas_call(
        functools.partial(_peer_body, tl=tl, te=te),
        grid=(t // tl, PEER_EXPERTS // te),
        in_specs=[tok(), tok(), tok(), keys(), keys(),
                  pl.BlockSpec((te, D_MODEL), lambda i, e: (e, 0)),
                  pl.BlockSpec((D_MODEL, te), lambda i, e: (0, e)),
                  pl.BlockSpec((1, D_MODEL), lambda i, e: (0, 0))],
        out_specs=tok(),
        out_shape=_sds((t, D_MODEL)),
        scratch_shapes=[sel(), sel(), sel(), sel(), pltpu.VMEM((2, PEER_TOPK, tl), F32),
                        pltpu.VMEM((D_MODEL, tl), F32)],
        compiler_params=_cparams("parallel", "arbitrary"),
        name="peer",
    )(qp, h2, x1, wts["peer_k1"], wts["peer_k2"], wts["peer_u"], wts["peer_vt"], wts["final_norm_g"])


def _pick(n, prefs):
    for p in prefs:
        if n % p == 0:
            return p
    return n


def _group(x3, k_arr, k_cb, v_arr, v_cb, s_gla0, s_gdn0, conv0, wts):
    nb, seq, d = x3.shape
    t = nb * seq
    x2d = x3.reshape(t, d)
    proj, small = _inproj(x2d, wts["norm_mix_g"], wts["w_main"], wts["w_small"],
                          tm=_pick(t, (1024, 512, 256, 128)), tn=1024)
    gla_ch = 16 if seq % 16 == 0 else SUBLANES
    gla_lt = _pick(seq, (256, 128, 64, 32, 16))
    o_a, s_gla = _gla(proj, small, wts, s_gla0, nb, seq, gla_ch, gla_lt)
    o_b, s_gdn, conv_new = _gdn(proj, small, wts, s_gdn0, conv0, nb, seq, _pick(seq, (64, 32, 16)))
    o_c = _xattn(proj, k_arr, k_cb, v_arr, v_cb, nb, seq, _pick(seq, (512, 256, 128)))
    merged = _merge(o_a, o_b, o_c, proj, wts, tm=_pick(t, (256, 128)))
    x1, h2 = _outproj(merged, wts["w_out"], x2d, wts["norm_ffn_g"], tm=_pick(t, (512, 256, 128)))
    qp = _peerq(h2, wts["peer_wq"], tm=_pick(t, (512, 256, 128)))
    y = _peer(qp, h2, x1, wts, tl=_pick(t, (512, 256, 128)), te=512)
    return y.reshape(nb, seq, d), s_gla, s_gdn, conv_new


def _lanes(vec, start):
    return jnp.zeros((1, SMALL_W), F32).at[0, start:start + vec.shape[0]].set(vec.astype(F32))


def kernel(x_prompt, x_sample, mem_prompt, cache_mem_k, cache_mem_v, state_gla, state_gdn, state_conv, norm_mix_g, norm_mem_g, w_in, w_gla_gate, b_gla_gate, gla_norm_g, gdn_conv_w, gdn_a_log, gdn_dt_bias, gdn_norm_g, w_mem_kv, w_br_gla, w_br_gdn, w_br_mem, b_gates, w_out, norm_ffn_g, peer_wq, peer_k1, peer_k2, peer_u, peer_v, final_norm_g):
    depth = w_in.shape[0]
    assert depth == 1, "the chain below is written for a single layer"
    l = 0
    off, cols = 0, {}
    for name, n in zip(IN_NAMES, IN_SPLITS):
        cols[name] = w_in[l][:, off:off + n]
        off += n
    w_main = jnp.concatenate([cols[n] for n in MAIN_NAMES], axis=1).astype(BF16)
    pad = jnp.zeros((D_MODEL, SMALL_W - GLA_RANK - 2 * GDN_HEADS), F32)
    w_small = jnp.concatenate([cols["glr"], cols["da"], cols["db"], pad], axis=1).astype(BF16)
    wts = {
        "norm_mix_g": norm_mix_g[l][None], "w_main": w_main, "w_small": w_small,
        "w_gla_gate": w_gla_gate[l], "b_gla_gate": b_gla_gate[l][None], "gla_norm_g": gla_norm_g[l][None],
        "gdn_conv_w": gdn_conv_w[l], "a_log_lanes": _lanes(gdn_a_log[l], DA_LANE),
        "dt_bias_lanes": _lanes(gdn_dt_bias[l], DA_LANE), "gdn_norm_g": gdn_norm_g[l][None],
        "head_sel": (lax.broadcasted_iota(I32, (GDN_HEADS, SMALL_W), 1)
                     == lax.broadcasted_iota(I32, (GDN_HEADS, SMALL_W), 0) + DA_LANE).astype(F32),
        "w_br_gla": w_br_gla[l].astype(BF16), "w_br_gdn": w_br_gdn[l].astype(BF16),
        "w_br_mem": w_br_mem[l].astype(BF16), "b_gates": b_gates[l][None], "w_out": w_out[l].astype(BF16),
        "norm_ffn_g": norm_ffn_g[l][None], "peer_wq": peer_wq[l].astype(BF16),
        "peer_k1": peer_k1[l].astype(BF16), "peer_k2": peer_k2[l].astype(BF16),
        "peer_u": peer_u[l].astype(BF16), "peer_vt": peer_v[l].T.astype(BF16),
        "final_norm_g": final_norm_g[None],
    }
    nb_p, n_mem, _ = mem_prompt.shape
    nb_s = x_sample.shape[0]

    kv = _normmm(mem_prompt.reshape(nb_p * n_mem, D_MODEL), norm_mem_g[l][None], w_mem_kv[l].astype(BF16),
                 tm=_pick(nb_p * n_mem, (512, 256)), tn=1024)
    kv3 = kv.reshape(nb_p, n_mem, 2 * MEM_W)
    y_p, gla_p, gdn_p, conv_p = _group(x_prompt, kv3, 0, kv3, 1, None, None, None, wts)
    mk_p = kv3[:, :, :MEM_W].reshape(nb_p, n_mem, MEM_HEADS, MEM_DH)
    mv_p = kv3[:, :, MEM_W:].reshape(nb_p, n_mem, MEM_HEADS, MEM_DH)

    ck = cache_mem_k[l].reshape(nb_s, n_mem, MEM_W)
    cv = cache_mem_v[l].reshape(nb_s, n_mem, MEM_W)
    y_s, gla_s, gdn_s, conv_s = _group(x_sample, ck, 0, cv, 0, state_gla[l], state_gdn[l], state_conv[l], wts)

    return (y_p, y_s, gla_p[None], gdn_p[None], conv_p[None], mk_p[None], mv_p[None],
            gla_s[None], gdn_s[None], conv_s[None])
```

```python
import functools

import jax
import jax.numpy as jnp
from jax import lax
from jax.experimental import pallas as pl
from jax.experimental.pallas import tpu as pltpu

F32 = jnp.float32
BF16 = jnp.bfloat16
I32 = jnp.int32
HI = lax.Precision.HIGHEST
NEG_INF = float("-inf")

D_MODEL = 2048
EPS = 1e-6
N_MEM = 256
GLA_HEADS, GLA_DK, GLA_DV, GLA_RANK, GLA_TAU = 4, 128, 256, 16, 16.0
GLA_QK, GLA_V = GLA_HEADS * GLA_DK, GLA_HEADS * GLA_DV
GDN_HEADS, GDN_DK, GDN_DV, GDN_CONV = 8, 128, 128, 4
GDN_QK, GDN_V = GDN_HEADS * GDN_DK, GDN_HEADS * GDN_DV
GDN_CONV_CH = 2 * GDN_QK + GDN_V
MEM_HEADS, MEM_DH = 4, 256
MEM_W = MEM_HEADS * MEM_DH
PEER_KEYS, PEER_HEADS, PEER_HALF, PEER_TOPK = 128, 8, 128, 16
PEER_EXPERTS = PEER_KEYS * PEER_KEYS
PEER_SUB = 256
N_GATES = 3 * D_MODEL
IN_SPLITS = (GLA_QK, GLA_QK, GLA_V, GLA_V, GLA_RANK, GDN_CONV_CH, GDN_V, GDN_HEADS, GDN_HEADS, MEM_W, N_GATES)
IN_NAMES = ("gq", "gk", "gv", "gr", "glr", "dqkv", "dz", "da", "db", "mq", "gates")
MAIN_NAMES = ("gq", "gk", "gv", "gr", "dqkv", "dz", "mq", "gates")
N_MAIN = 2 * GLA_QK + 2 * GLA_V + GDN_CONV_CH + GDN_V + MEM_W + N_GATES
SMALL_W = 128
DA_LANE, DB_LANE = GLA_RANK, GLA_RANK + GDN_HEADS

VMEM_LIMIT_BYTES = 56 * 1024 * 1024
SUBLANES, LANES = 8, 128


def _cparams(*sem):
    return pltpu.CompilerParams(dimension_semantics=sem, vmem_limit_bytes=VMEM_LIMIT_BYTES)


def _sds(shape, dtype=F32):
    return jax.ShapeDtypeStruct(shape, dtype)


def _mm(a, b, **kw):
    return jnp.dot(a, b, preferred_element_type=F32, **kw)


def _nt(a, b, **kw):
    return lax.dot_general(a, b, (((1,), (1,)), ((), ())), preferred_element_type=F32, **kw)


def _tn(a, b):
    return lax.dot_general(a, b, (((0,), (0,)), ((), ())), preferred_element_type=F32)


def _mm3(a, b):
    ah = a.astype(BF16).astype(F32)
    bh = b.astype(BF16).astype(F32)
    return _mm(ah, bh) + (_mm(ah, b - bh) + _mm(a - ah, bh))


def _softplus(x):
    return jnp.maximum(x, 0.0) + jnp.log1p(jnp.exp(-jnp.abs(x)))


def _silu(x):
    return x * jax.nn.sigmoid(x)


def _rms(x, g):
    return x * lax.rsqrt(jnp.mean(x * x, axis=-1, keepdims=True) + EPS) * g


def _inproj_body(x_ref, g_ref, w_ref, ws_ref, o_ref, os_ref, xn_ref):
    @pl.when(pl.program_id(1) == 0)
    def _():
        xn = _rms(x_ref[...], g_ref[...]).astype(BF16)
        xn_ref[...] = xn
        os_ref[...] = _mm(xn, ws_ref[...])

    o_ref[...] = _mm(xn_ref[...], w_ref[...])


def _inproj(x2d, g, w_main, w_small, tm, tn):
    t, k = x2d.shape
    n = w_main.shape[1]
    return pl.pallas_call(
        _inproj_body,
        grid=(t // tm, n // tn),
        in_specs=[pl.BlockSpec((tm, k), lambda i, j: (i, 0)),
                  pl.BlockSpec((1, k), lambda i, j: (0, 0)),
                  pl.BlockSpec((k, tn), lambda i, j: (0, j)),
                  pl.BlockSpec((k, SMALL_W), lambda i, j: (0, 0))],
        out_specs=[pl.BlockSpec((tm, tn), lambda i, j: (i, j)),
                   pl.BlockSpec((tm, SMALL_W), lambda i, j: (i, 0))],
        out_shape=[_sds((t, n)), _sds((t, SMALL_W))],
        scratch_shapes=[pltpu.VMEM((tm, k), BF16)],
        compiler_params=_cparams("parallel", "arbitrary"),
        name="inproj",
    )(x2d, g, w_main, w_small)


def _normmm_body(x_ref, g_ref, w_ref, o_ref, xn_ref):
    @pl.when(pl.program_id(1) == 0)
    def _():
        xn_ref[...] = _rms(x_ref[...], g_ref[...]).astype(BF16)

    o_ref[...] = _mm(xn_ref[...], w_ref[...])


def _normmm(x2d, g, w, tm, tn):
    t, k = x2d.shape
    n = w.shape[1]
    return pl.pallas_call(
        _normmm_body,
        grid=(t // tm, n // tn),
        in_specs=[pl.BlockSpec((tm, k), lambda i, j: (i, 0)),
                  pl.BlockSpec((1, k), lambda i, j: (0, 0)),
                  pl.BlockSpec((k, tn), lambda i, j: (0, j))],
        out_specs=pl.BlockSpec((tm, tn), lambda i, j: (i, j)),
        out_shape=_sds((t, n)),
        scratch_shapes=[pltpu.VMEM((tm, k), BF16)],
        compiler_params=_cparams("parallel", "arbitrary"),
        name="memkv",
    )(x2d, g, w)


def _gla_body(*refs, ch, nc, has_state):
    if has_state:
        (q_ref, k_ref, v_ref, r_ref, sm_ref, wg_ref, bg_ref, ng_ref, tri_ref, s0_ref,
         o_ref, so_ref, st_scr, b_scr) = refs
    else:
        (q_ref, k_ref, v_ref, r_ref, sm_ref, wg_ref, bg_ref, ng_ref, tri_ref,
         o_ref, so_ref, st_scr, b_scr) = refs
    step = pl.program_id(1)

    @pl.when(step == 0)
    def _():
        for h in range(GLA_HEADS):
            if has_state:
                st_scr[h] = s0_ref[0, h].T
            else:
                st_scr[h] = jnp.zeros((GLA_DV, GLA_DK), F32)

    logit = _mm(sm_ref[:, 0:GLA_RANK], wg_ref[...], precision=HI) + bg_ref[...]
    g = (jnp.minimum(logit, 0.0) - jnp.log1p(jnp.exp(-jnp.abs(logit)))) * (1.0 / GLA_TAU)
    b_scr[...] = _mm(tri_ref[...], g, precision=HI)

    row_io = lax.broadcasted_iota(I32, (ch, GLA_DK), 0)
    lane_io = lax.broadcasted_iota(I32, (ch, ch), 1)

    def chunk(c, carry):
        rows = pl.ds(pl.multiple_of(c * ch, ch), ch)
        for h in range(GLA_HEADS):
            ksl = slice(h * GLA_DK, (h + 1) * GLA_DK)
            vsl = slice(h * GLA_DV, (h + 1) * GLA_DV)
            qh = q_ref[rows, ksl] * (GLA_DK ** -0.5)
            kh = k_ref[rows, ksl]
            vh = v_ref[rows, vsl]
            bh = b_scr[rows, ksl]
            st = st_scr[h]
            o = _nt(qh * jnp.exp(bh), st)
            att_t = jnp.zeros((ch, ch), F32)
            for i in range(ch):
                dec = jnp.exp(jnp.where(row_io <= i, bh[i:i + 1, :] - bh, NEG_INF))
                col = jnp.sum(kh * dec * qh[i:i + 1, :], axis=1, keepdims=True)
                att_t = jnp.where(lane_io == i, col, att_t)
            o = o + _tn(att_t, vh)
            b_last = bh[ch - 1:ch, :]
            st_scr[h] = jnp.exp(b_last) * st + _tn(vh, kh * jnp.exp(b_last - bh))
            rh = r_ref[rows, vsl]
            o_ref[rows, vsl] = _rms(o, ng_ref[...]) * _silu(rh)
        return carry

    lax.fori_loop(0, nc, chunk, 0)

    @pl.when(step == pl.num_programs(1) - 1)
    def _():
        for h in range(GLA_HEADS):
            so_ref[0, h] = st_scr[h].T


def _gla(proj, small, wts, s0, nb, seq, ch, lt):
    nsteps = seq // lt
    nc = lt // ch
    t = nb * seq
    rix = lambda b, s: b * nsteps + s
    idx = lax.broadcasted_iota(I32, (lt, lt), 0)
    jdx = lax.broadcasted_iota(I32, (lt, lt), 1)
    tri = ((jdx <= idx) & (idx // ch == jdx // ch)).astype(F32)
    in_specs = [pl.BlockSpec((lt, GLA_QK), lambda b, s: (rix(b, s), 0)),
                pl.BlockSpec((lt, GLA_QK), lambda b, s: (rix(b, s), 1)),
                pl.BlockSpec((lt, GLA_V), lambda b, s: (rix(b, s), 1)),
                pl.BlockSpec((lt, GLA_V), lambda b, s: (rix(b, s), 2)),
                pl.BlockSpec((lt, SMALL_W), lambda b, s: (rix(b, s), 0)),
                pl.BlockSpec((GLA_RANK, GLA_QK), lambda b, s: (0, 0)),
                pl.BlockSpec((1, GLA_QK), lambda b, s: (0, 0)),
                pl.BlockSpec((1, GLA_DV), lambda b, s: (0, 0)),
                pl.BlockSpec((lt, lt), lambda b, s: (0, 0))]
    args = [proj, proj, proj, proj, small, wts["w_gla_gate"], wts["b_gla_gate"], wts["gla_norm_g"], tri]
    if s0 is not None:
        in_specs.append(pl.BlockSpec((1, GLA_HEADS, GLA_DK, GLA_DV), lambda b, s: (b, 0, 0, 0)))
        args.append(s0)
    return pl.pallas_call(
        functools.partial(_gla_body, ch=ch, nc=nc, has_state=s0 is not None),
        grid=(nb, nsteps),
        in_specs=in_specs,
        out_specs=[pl.BlockSpec((lt, GLA_V), lambda b, s: (rix(b, s), 0)),
                   pl.BlockSpec((1, GLA_HEADS, GLA_DK, GLA_DV), lambda b, s: (b, 0, 0, 0))],
        out_shape=[_sds((t, GLA_V)), _sds((nb, GLA_HEADS, GLA_DK, GLA_DV))],
        scratch_shapes=[pltpu.VMEM((GLA_HEADS, GLA_DV, GLA_DK), F32), pltpu.VMEM((lt, GLA_QK), F32)],
        compiler_params=_cparams("parallel", "arbitrary"),
        name="gla",
    )(*args)


def _gdn_body(*refs, c, has_state, nsq):
    if has_state:
        (x_ref, z_ref, sm_ref, cw_ref, al_ref, dtb_ref, ng_ref, sel_ref, s0_ref, c0_ref,
         o_ref, so_ref, co_ref, st_scr, xp_scr) = refs
    else:
        (x_ref, z_ref, sm_ref, cw_ref, al_ref, dtb_ref, ng_ref, sel_ref,
         o_ref, so_ref, co_ref, st_scr, xp_scr) = refs
    step = pl.program_id(1)
    hist = GDN_CONV - 1
    base = SUBLANES - hist

    @pl.when(step == 0)
    def _():
        if has_state:
            st_scr[...] = s0_ref[0]
            xp_scr[base:SUBLANES, :] = c0_ref[0]
        else:
            st_scr[...] = jnp.zeros(st_scr.shape, F32)
            xp_scr[base:SUBLANES, :] = jnp.zeros((hist, GDN_CONV_CH), F32)

    xp_scr[SUBLANES:SUBLANES + c, :] = x_ref[...]
    conv = xp_scr[pl.ds(base, c), :] * cw_ref[0:1, :]
    for w in range(1, GDN_CONV):
        conv = conv + xp_scr[pl.ds(base + w, c), :] * cw_ref[w:w + 1, :]
    tail = xp_scr[pl.ds(c + base, hist), :]
    xp_scr[base:SUBLANES, :] = tail
    co_ref[0] = tail
    act = _silu(conv)

    sm = sm_ref[...]
    gdec = -jnp.exp(al_ref[...]) * _softplus(sm + dtb_ref[...])
    beta = jax.nn.sigmoid(sm)
    ii = lax.broadcasted_iota(I32, (c, c), 0)
    jj = lax.broadcasted_iota(I32, (c, c), 1)
    bcol = _mm((ii >= jj).astype(F32), gdec, precision=HI)
    brow = _nt(sel_ref[...], bcol, precision=HI)
    eye = (ii == jj).astype(F32)

    for h in range(GDN_HEADS):
        sl = slice(h * GDN_DK, (h + 1) * GDN_DK)
        cq = act[:, h * GDN_DK:(h + 1) * GDN_DK]
        ck = act[:, GDN_QK + h * GDN_DK:GDN_QK + (h + 1) * GDN_DK]
        vh = act[:, 2 * GDN_QK + h * GDN_DV:2 * GDN_QK + (h + 1) * GDN_DV]
        qh = cq * lax.rsqrt(jnp.sum(cq * cq, axis=-1, keepdims=True) + EPS) * (GDN_DK ** -0.5)
        kh = ck * lax.rsqrt(jnp.sum(ck * ck, axis=-1, keepdims=True) + EPS)
        bc = bcol[:, DA_LANE + h:DA_LANE + h + 1]
        br = brow[h:h + 1, :]
        bt = beta[:, DB_LANE + h:DB_LANE + h + 1]
        gam = jnp.exp(jnp.where(ii >= jj, bc - br, NEG_INF))
        kk = _nt(kh, kh)
        qk = _nt(qh, kh)
        p = -(bt * jnp.where(ii > jj, gam, 0.0) * kk)
        ainv = eye + p
        pk = p
        for _ in range(nsq):
            pk = _mm3(pk, pk)
            ainv = ainv + _mm3(ainv, pk)
        s_h = st_scr[h]
        ks = _mm(kh, s_h)
        qs = _mm(qh, s_h)
        eb = jnp.exp(bc)
        u = _mm3(ainv, bt * (vh - eb * ks))
        o = eb * qs + _mm(qk * gam, u)
        b_last = bc[c - 1:c, :]
        st_scr[h] = jnp.exp(b_last) * s_h + _tn(kh * jnp.exp(b_last - bc), u)
        o_ref[:, sl] = _rms(o, ng_ref[...]) * _silu(z_ref[:, sl])

    @pl.when(step == pl.num_programs(1) - 1)
    def _():
        so_ref[0] = st_scr[...]


def _gdn(proj, small, wts, s0, conv0, nb, seq, c):
    nsteps = seq // c
    t = nb * seq
    rix = lambda b, s: b * nsteps + s
    nsq = max(c.bit_length() - 2, 0)
    in_specs = [pl.BlockSpec((c, GDN_CONV_CH), lambda b, s: (rix(b, s), 1)),
                pl.BlockSpec((c, GDN_V), lambda b, s: (rix(b, s), 6)),
                pl.BlockSpec((c, SMALL_W), lambda b, s: (rix(b, s), 0)),
                pl.BlockSpec((GDN_CONV, GDN_CONV_CH), lambda b, s: (0, 0)),
                pl.BlockSpec((1, SMALL_W), lambda b, s: (0, 0)),
                pl.BlockSpec((1, SMALL_W), lambda b, s: (0, 0)),
                pl.BlockSpec((1, GDN_DV), lambda b, s: (0, 0)),
                pl.BlockSpec((GDN_HEADS, SMALL_W), lambda b, s: (0, 0))]
    args = [proj, proj, small, wts["gdn_conv_w"], wts["a_log_lanes"], wts["dt_bias_lanes"], wts["gdn_norm_g"],
            wts["head_sel"]]
    if s0 is not None:
        in_specs += [pl.BlockSpec((1, GDN_HEADS, GDN_DK, GDN_DV), lambda b, s: (b, 0, 0, 0)),
                     pl.BlockSpec((1, GDN_CONV - 1, GDN_CONV_CH), lambda b, s: (b, 0, 0))]
        args += [s0, conv0]
    return pl.pallas_call(
        functools.partial(_gdn_body, c=c, has_state=s0 is not None, nsq=nsq),
        grid=(nb, nsteps),
        in_specs=in_specs,
        out_specs=[pl.BlockSpec((c, GDN_V), lambda b, s: (rix(b, s), 0)),
                   pl.BlockSpec((1, GDN_HEADS, GDN_DK, GDN_DV), lambda b, s: (b, 0, 0, 0)),
                   pl.BlockSpec((1, GDN_CONV - 1, GDN_CONV_CH), lambda b, s: (b, 0, 0))],
        out_shape=[_sds((t, GDN_V)), _sds((nb, GDN_HEADS, GDN_DK, GDN_DV)), _sds((nb, GDN_CONV - 1, GDN_CONV_CH))],
        scratch_shapes=[pltpu.VMEM((GDN_HEADS, GDN_DK, GDN_DV), F32),
                        pltpu.VMEM((c + SUBLANES, GDN_CONV_CH), F32)],
        compiler_params=_cparams("parallel", "arbitrary"),
        name="gdn",
    )(*args)


def _xattn_body(q_ref, k_ref, v_ref, o_ref):
    for h in range(MEM_HEADS):
        sl = slice(h * MEM_DH, (h + 1) * MEM_DH)
        sc = _nt(q_ref[:, sl], k_ref[0, :, sl]) * (MEM_DH ** -0.5)
        p = jnp.exp(sc - jnp.max(sc, axis=-1, keepdims=True))
        p = p / jnp.sum(p, axis=-1, keepdims=True)
        o_ref[:, sl] = _mm(p, v_ref[0, :, sl])


def _xattn(proj, k_arr, k_cb, v_arr, v_cb, nb, seq, lt):
    nsteps = seq // lt
    t = nb * seq
    return pl.pallas_call(
        _xattn_body,
        grid=(nb, nsteps),
        in_specs=[pl.BlockSpec((lt, MEM_W), lambda b, s: (b * nsteps + s, 7)),
                  pl.BlockSpec((1, N_MEM, MEM_W), lambda b, s: (b, 0, k_cb)),
                  pl.BlockSpec((1, N_MEM, MEM_W), lambda b, s: (b, 0, v_cb))],
        out_specs=pl.BlockSpec((lt, MEM_W), lambda b, s: (b * nsteps + s, 0)),
        out_shape=_sds((t, MEM_W)),
        compiler_params=_cparams("parallel", "arbitrary"),
        name="xattn",
    )(proj, k_arr, v_arr)


def _merge_body(oa_ref, ob_ref, oc_ref, ga_ref, gb_ref, gc_ref, ba_ref, bb_ref, bc_ref,
                wa_ref, wb_ref, wc_ref, m_ref):
    acc = jax.nn.sigmoid(ga_ref[...] + ba_ref[...]) * _mm(oa_ref[...].astype(BF16), wa_ref[...])
    acc = acc + jax.nn.sigmoid(gb_ref[...] + bb_ref[...]) * _mm(ob_ref[...].astype(BF16), wb_ref[...])
    acc = acc + jax.nn.sigmoid(gc_ref[...] + bc_ref[...]) * _mm(oc_ref[...].astype(BF16), wc_ref[...])
    m_ref[...] = acc.astype(BF16)


def _merge(o_a, o_b, o_c, proj, wts, tm):
    t = o_a.shape[0]
    row = lambda w: pl.BlockSpec((tm, w), lambda i: (i, 0))
    gate = lambda cb: pl.BlockSpec((tm, D_MODEL), lambda i: (i, cb))
    bias = lambda cb: pl.BlockSpec((1, D_MODEL), lambda i: (0, cb))
    wspec = lambda k: pl.BlockSpec((k, D_MODEL), lambda i: (0, 0))
    return pl.pallas_call(
        _merge_body,
        grid=(t // tm,),
        in_specs=[row(GLA_V), row(GDN_V), row(MEM_W), gate(4), gate(5), gate(6), bias(0), bias(1), bias(2),
                  wspec(GLA_V), wspec(GDN_V), wspec(MEM_W)],
        out_specs=pl.BlockSpec((tm, D_MODEL), lambda i: (i, 0)),
        out_shape=_sds((t, D_MODEL), BF16),
        compiler_params=_cparams("parallel"),
        name="merge",
    )(o_a, o_b, o_c, proj, proj, proj, wts["b_gates"], wts["b_gates"], wts["b_gates"],
      wts["w_br_gla"], wts["w_br_gdn"], wts["w_br_mem"])


def _outproj_body(m_ref, w_ref, x_ref, g_ref, x1_ref, h2_ref):
    x1 = x_ref[...] + _mm(m_ref[...], w_ref[...])
    x1_ref[...] = x1
    h2_ref[...] = _rms(x1, g_ref[...]).astype(BF16)


def _outproj(merged, w_out, x2d, g, tm):
    t = x2d.shape[0]
    blk = lambda: pl.BlockSpec((tm, D_MODEL), lambda i: (i, 0))
    return pl.pallas_call(
        _outproj_body,
        grid=(t // tm,),
        in_specs=[blk(), pl.BlockSpec((D_MODEL, D_MODEL), lambda i: (0, 0)), blk(),
                  pl.BlockSpec((1, D_MODEL), lambda i: (0, 0))],
        out_specs=[blk(), blk()],
        out_shape=[_sds((t, D_MODEL)), _sds((t, D_MODEL), BF16)],
        compiler_params=_cparams("parallel"),
        name="outproj",
    )(merged, w_out, x2d, g)


def _peerq_body(h_ref, w_ref, q_ref):
    q_ref[...] = _mm(h_ref[...], w_ref[...]).astype(BF16)


def _peerq(h2, wq, tm):
    t = h2.shape[0]
    n = wq.shape[1]
    return pl.pallas_call(
        _peerq_body,
        grid=(t // tm,),
        in_specs=[pl.BlockSpec((tm, D_MODEL), lambda i: (i, 0)), pl.BlockSpec((D_MODEL, n), lambda i: (0, 0))],
        out_specs=pl.BlockSpec((tm, n), lambda i: (i, 0)),
        out_shape=_sds((t, n), BF16),
        compiler_params=_cparams("parallel"),
        name="peerq",
    )(h2, wq)


def _top16(x, tv_ref, slot):
    io = lax.broadcasted_iota(I32, x.shape, 0)
    rank = jnp.full(x.shape, PEER_TOPK, I32)
    for r in range(PEER_TOPK):
        m = jnp.max(x, axis=0, keepdims=True)
        idx = jnp.min(jnp.where(x == m, io, x.shape[0]), axis=0, keepdims=True)
        hit = io == idx
        rank = jnp.where(hit, r, rank)
        x = jnp.where(hit, NEG_INF, x)
        tv_ref[slot, r:r + 1, :] = m
    return rank


def _peer_select(h, qp_ref, k1_ref, k2_ref, tv_scr, lh_scr, p1_scr, r2_scr, p2_scr, tl):
    base = h * 2 * PEER_HALF
    s1 = _nt(k1_ref[...], qp_ref[:, base:base + PEER_HALF])
    s2 = _nt(k2_ref[...], qp_ref[:, base + PEER_HALF:base + 2 * PEER_HALF])
    rank1 = _top16(s1, tv_scr, 0)
    rank2 = _top16(s2, tv_scr, 1)
    t1 = tv_scr[0]
    t2 = tv_scr[1]
    cand = jnp.concatenate([t1[a:a + 1, :] + t2 for a in range(PEER_TOPK)], axis=0)
    code = lax.broadcasted_iota(I32, cand.shape, 0)
    a_io = lax.broadcasted_iota(I32, (PEER_TOPK, tl), 0)
    limit = jnp.zeros((PEER_TOPK, tl), I32)
    for _ in range(PEER_TOPK):
        m = jnp.max(cand, axis=0, keepdims=True)
        idx = jnp.min(jnp.where(cand == m, code, cand.shape[0]), axis=0, keepdims=True)
        cand = jnp.where(code == idx, NEG_INF, cand)
        limit = limit + (a_io == (idx >> 4)).astype(I32)
    p1s = jnp.exp(t1 - t1[0:1, :])
    p2s = jnp.exp(t2 - t2[0:1, :])
    inner = jnp.zeros((PEER_TOPK, tl), F32)
    for b in range(PEER_TOPK):
        inner = inner + jnp.where(limit > b, p2s[b:b + 1, :], 0.0)
    inv_z = 1.0 / jnp.sum(p1s * inner, axis=0, keepdims=True)
    lim_e1 = jnp.zeros(rank1.shape, I32)
    for a in range(PEER_TOPK):
        lim_e1 = jnp.where(rank1 == a, limit[a:a + 1, :], lim_e1)
    lh_scr[h] = lim_e1.astype(F32)
    p1_scr[h] = jnp.where(rank1 < PEER_TOPK, jnp.exp(s1 - t1[0:1, :]), 0.0) * inv_z
    r2_scr[h] = rank2.astype(F32)
    p2_scr[h] = jnp.exp(jnp.minimum(s2 - t2[0:1, :], 0.0))


def _peer_dense(tile, u_ref, vt_ref, h2_ref, lh_scr, p1_scr, r2_scr, p2_scr, acc_scr, z_prev, z_next, tl, te):
    acc_scr[...] += _mm(vt_ref[...], z_prev[...])
    e1_per_sub = PEER_SUB // PEER_KEYS
    for s in range(te // PEER_SUB):
        sc = _nt(u_ref[s * PEER_SUB:(s + 1) * PEER_SUB, :], h2_ref[...])
        act = 0.5 * sc * (1.0 + lax.erf(sc * 0.7071067811865476))
        for i in range(e1_per_sub):
            e1 = tile * (te // PEER_KEYS) + s * e1_per_sub + i
            coef = jnp.zeros((PEER_KEYS, tl), F32)
            for h in range(PEER_HEADS):
                lim = lh_scr[h, pl.ds(e1, 1), :]
                p1 = p1_scr[h, pl.ds(e1, 1), :]
                coef = coef + jnp.where(r2_scr[h] < lim, p2_scr[h], 0.0) * p1
            row = s * PEER_SUB + i * PEER_KEYS
            z_next[row:row + PEER_KEYS, :] = (coef * act[i * PEER_KEYS:(i + 1) * PEER_KEYS, :]).astype(BF16)


def _peer_body(qp_ref, h2_ref, x1_ref, k1_ref, k2_ref, u_ref, vt_ref, fg_ref, y_ref,
               lh_scr, p1_scr, r2_scr, p2_scr, tv_scr, acc_scr, za_scr, zb_scr, *, tl, te, ne):
    e = pl.program_id(1)

    @pl.when(e == 0)
    def _():
        acc_scr[...] = jnp.zeros(acc_scr.shape, F32)
        zb_scr[...] = jnp.zeros(zb_scr.shape, BF16)
        for h in range(PEER_HEADS):
            _peer_select(h, qp_ref, k1_ref, k2_ref, tv_scr, lh_scr, p1_scr, r2_scr, p2_scr, tl)

    tile = jnp.minimum(e, ne - 1)
    sel = (lh_scr, p1_scr, r2_scr, p2_scr)

    @pl.when(e % 2 == 0)
    def _():
        _peer_dense(tile, u_ref, vt_ref, h2_ref, *sel, acc_scr, zb_scr, za_scr, tl, te)

    @pl.when(e % 2 == 1)
    def _():
        _peer_dense(tile, u_ref, vt_ref, h2_ref, *sel, acc_scr, za_scr, zb_scr, tl, te)

    @pl.when(e == ne)
    def _():
        y_ref[...] = _rms(x1_ref[...] + acc_scr[...].T, fg_ref[...])


def _peer(qp, h2, x1, wts, tl, te):
    t = h2.shape[0]
    ne = PEER_EXPERTS // te
    assert ne % 2 == 0, "the drain step must read the buffer the last odd step wrote"
    once = pl.Buffered(1)
    keys = lambda: pl.BlockSpec((PEER_KEYS, PEER_HALF), lambda i, e: (0, 0))
    sel = lambda: pltpu.VMEM((PEER_HEADS, PEER_KEYS, tl), F32)
    return pl.pallas_call(
        functools.partial(_peer_body, tl=tl, te=te, ne=ne),
        grid=(t // tl, ne + 1),
        in_specs=[pl.BlockSpec((tl, D_MODEL), lambda i, e: (i, 0), pipeline_mode=once),
                  pl.BlockSpec((tl, D_MODEL), lambda i, e: (i, 0)),
                  pl.BlockSpec((tl, D_MODEL), lambda i, e: (i, 0), pipeline_mode=once),
                  keys(), keys(),
                  pl.BlockSpec((te, D_MODEL), lambda i, e: (jnp.minimum(e, ne - 1), 0)),
                  pl.BlockSpec((D_MODEL, te), lambda i, e: (0, jnp.maximum(e - 1, 0))),
                  pl.BlockSpec((1, D_MODEL), lambda i, e: (0, 0))],
        out_specs=pl.BlockSpec((tl, D_MODEL), lambda i, e: (i, 0)),
        out_shape=_sds((t, D_MODEL)),
        scratch_shapes=[sel(), sel(), sel(), sel(), pltpu.VMEM((2, PEER_TOPK, tl), F32),
                        pltpu.VMEM((D_MODEL, tl), F32), pltpu.VMEM((te, tl), BF16), pltpu.VMEM((te, tl), BF16)],
        compiler_params=_cparams("parallel", "arbitrary"),
        name="peer",
    )(qp, h2, x1, wts["peer_k1"], wts["peer_k2"], wts["peer_u"], wts["peer_vt"], wts["final_norm_g"])


def _pick(n, prefs):
    for p in prefs:
        if n % p == 0:
            return p
    return n


def _group(x3, k_arr, k_cb, v_arr, v_cb, s_gla0, s_gdn0, conv0, wts):
    nb, seq, d = x3.shape
    t = nb * seq
    x2d = x3.reshape(t, d)
    proj, small = _inproj(x2d, wts["norm_mix_g"], wts["w_main"], wts["w_small"],
                          tm=_pick(t, (1024, 512, 256, 128)), tn=1024)
    gla_ch = 16 if seq % 16 == 0 else SUBLANES
    gla_lt = _pick(seq, (256, 128, 64, 32, 16))
    o_a, s_gla = _gla(proj, small, wts, s_gla0, nb, seq, gla_ch, gla_lt)
    o_b, s_gdn, conv_new = _gdn(proj, small, wts, s_gdn0, conv0, nb, seq, _pick(seq, (64, 32, 16)))
    o_c = _xattn(proj, k_arr, k_cb, v_arr, v_cb, nb, seq, _pick(seq, (512, 256, 128)))
    merged = _merge(o_a, o_b, o_c, proj, wts, tm=_pick(t, (256, 128)))
    x1, h2 = _outproj(merged, wts["w_out"], x2d, wts["norm_ffn_g"], tm=_pick(t, (512, 256, 128)))
    qp = _peerq(h2, wts["peer_wq"], tm=_pick(t, (512, 256, 128)))
    y = _peer(qp, h2, x1, wts, tl=_pick(t, (512, 256, 128)), te=1024)
    return y.reshape(nb, seq, d), s_gla, s_gdn, conv_new


def _lanes(vec, start):
    return jnp.zeros((1, SMALL_W), F32).at[0, start:start + vec.shape[0]].set(vec.astype(F32))


def kernel(x_prompt, x_sample, mem_prompt, cache_mem_k, cache_mem_v, state_gla, state_gdn, state_conv, norm_mix_g, norm_mem_g, w_in, w_gla_gate, b_gla_gate, gla_norm_g, gdn_conv_w, gdn_a_log, gdn_dt_bias, gdn_norm_g, w_mem_kv, w_br_gla, w_br_gdn, w_br_mem, b_gates, w_out, norm_ffn_g, peer_wq, peer_k1, peer_k2, peer_u, peer_v, final_norm_g):
    depth = w_in.shape[0]
    assert depth == 1, "the chain below is written for a single layer"
    l = 0
    off, cols = 0, {}
    for name, n in zip(IN_NAMES, IN_SPLITS):
        cols[name] = w_in[l][:, off:off + n]
        off += n
    w_main = jnp.concatenate([cols[n] for n in MAIN_NAMES], axis=1).astype(BF16)
    pad = jnp.zeros((D_MODEL, SMALL_W - GLA_RANK - 2 * GDN_HEADS), F32)
    w_small = jnp.concatenate([cols["glr"], cols["da"], cols["db"], pad], axis=1).astype(BF16)
    wts = {
        "norm_mix_g": norm_mix_g[l][None], "w_main": w_main, "w_small": w_small,
        "w_gla_gate": w_gla_gate[l], "b_gla_gate": b_gla_gate[l][None], "gla_norm_g": gla_norm_g[l][None],
        "gdn_conv_w": gdn_conv_w[l], "a_log_lanes": _lanes(gdn_a_log[l], DA_LANE),
        "dt_bias_lanes": _lanes(gdn_dt_bias[l], DA_LANE), "gdn_norm_g": gdn_norm_g[l][None],
        "head_sel": (lax.broadcasted_iota(I32, (GDN_HEADS, SMALL_W), 1)
                     == lax.broadcasted_iota(I32, (GDN_HEADS, SMALL_W), 0) + DA_LANE).astype(F32),
        "w_br_gla": w_br_gla[l].astype(BF16), "w_br_gdn": w_br_gdn[l].astype(BF16),
        "w_br_mem": w_br_mem[l].astype(BF16), "b_gates": b_gates[l][None], "w_out": w_out[l].astype(BF16),
        "norm_ffn_g": norm_ffn_g[l][None], "peer_wq": peer_wq[l].astype(BF16),
        "peer_k1": peer_k1[l].astype(BF16), "peer_k2": peer_k2[l].astype(BF16),
        "peer_u": peer_u[l].astype(BF16), "peer_vt": peer_v[l].T.astype(BF16),
        "final_norm_g": final_norm_g[None],
    }
    nb_p, n_mem, _ = mem_prompt.shape
    nb_s = x_sample.shape[0]

    kv = _normmm(mem_prompt.reshape(nb_p * n_mem, D_MODEL), norm_mem_g[l][None], w_mem_kv[l].astype(BF16),
                 tm=_pick(nb_p * n_mem, (512, 256)), tn=1024)
    kv3 = kv.reshape(nb_p, n_mem, 2 * MEM_W)
    y_p, gla_p, gdn_p, conv_p = _group(x_prompt, kv3, 0, kv3, 1, None, None, None, wts)
    mk_p = kv3[:, :, :MEM_W].reshape(nb_p, n_mem, MEM_HEADS, MEM_DH)
    mv_p = kv3[:, :, MEM_W:].reshape(nb_p, n_mem, MEM_HEADS, MEM_DH)

    ck = cache_mem_k[l].reshape(nb_s, n_mem, MEM_W)
    cv = cache_mem_v[l].reshape(nb_s, n_mem, MEM_W)
    y_s, gla_s, gdn_s, conv_s = _group(x_sample, ck, 0, cv, 0, state_gla[l], state_gdn[l], state_conv[l], wts)

    return (y_p, y_s, gla_p[None], gdn_p[None], conv_p[None], mk_p[None], mv_p[None],
            gla_s[None], gdn_s[None], conv_s[None])
```

```python
import functools

import jax
import jax.numpy as jnp
from jax import lax
from jax.experimental import pallas as pl
from jax.experimental.pallas import tpu as pltpu

F32 = jnp.float32
BF16 = jnp.bfloat16
I32 = jnp.int32
HI = lax.Precision.HIGHEST
NEG_INF = float("-inf")

D_MODEL = 2048
EPS = 1e-6
N_MEM = 256
GLA_HEADS, GLA_DK, GLA_DV, GLA_RANK, GLA_TAU = 4, 128, 256, 16, 16.0
GLA_QK, GLA_V = GLA_HEADS * GLA_DK, GLA_HEADS * GLA_DV
GDN_HEADS, GDN_DK, GDN_DV, GDN_CONV = 8, 128, 128, 4
GDN_QK, GDN_V = GDN_HEADS * GDN_DK, GDN_HEADS * GDN_DV
GDN_CONV_CH = 2 * GDN_QK + GDN_V
MEM_HEADS, MEM_DH = 4, 256
MEM_W = MEM_HEADS * MEM_DH
PEER_KEYS, PEER_HEADS, PEER_HALF, PEER_TOPK = 128, 8, 128, 16
PEER_EXPERTS = PEER_KEYS * PEER_KEYS
PEER_SUB = 256
N_GATES = 3 * D_MODEL
IN_SPLITS = (GLA_QK, GLA_QK, GLA_V, GLA_V, GLA_RANK, GDN_CONV_CH, GDN_V, GDN_HEADS, GDN_HEADS, MEM_W, N_GATES)
IN_NAMES = ("gq", "gk", "gv", "gr", "glr", "dqkv", "dz", "da", "db", "mq", "gates")
MAIN_NAMES = ("gq", "gk", "gv", "gr", "dqkv", "dz", "mq", "gates")
N_MAIN = 2 * GLA_QK + 2 * GLA_V + GDN_CONV_CH + GDN_V + MEM_W + N_GATES
SMALL_W = 128
DA_LANE, DB_LANE = GLA_RANK, GLA_RANK + GDN_HEADS

VMEM_LIMIT_BYTES = 56 * 1024 * 1024
SUBLANES, LANES = 8, 128


def _cparams(*sem):
    return pltpu.CompilerParams(dimension_semantics=sem, vmem_limit_bytes=VMEM_LIMIT_BYTES)


def _sds(shape, dtype=F32):
    return jax.ShapeDtypeStruct(shape, dtype)


def _mm(a, b, **kw):
    return jnp.dot(a, b, preferred_element_type=F32, **kw)


def _nt(a, b, **kw):
    return lax.dot_general(a, b, (((1,), (1,)), ((), ())), preferred_element_type=F32, **kw)


def _tn(a, b):
    return lax.dot_general(a, b, (((0,), (0,)), ((), ())), preferred_element_type=F32)


def _softplus(x):
    return jnp.maximum(x, 0.0) + jnp.log1p(jnp.exp(-jnp.abs(x)))


def _silu(x):
    return x * jax.nn.sigmoid(x)


def _rms(x, g):
    return x * lax.rsqrt(jnp.mean(x * x, axis=-1, keepdims=True) + EPS) * g


def _inproj_body(x_ref, g_ref, w_ref, ws_ref, o_ref, os_ref, xn_ref):
    @pl.when(pl.program_id(1) == 0)
    def _():
        xn = _rms(x_ref[...], g_ref[...]).astype(BF16)
        xn_ref[...] = xn
        os_ref[...] = _mm(xn, ws_ref[...])

    o_ref[...] = _mm(xn_ref[...], w_ref[...])


def _inproj(x2d, g, w_main, w_small, tm, tn):
    t, k = x2d.shape
    n = w_main.shape[1]
    return pl.pallas_call(
        _inproj_body,
        grid=(t // tm, n // tn),
        in_specs=[pl.BlockSpec((tm, k), lambda i, j: (i, 0)),
                  pl.BlockSpec((1, k), lambda i, j: (0, 0)),
                  pl.BlockSpec((k, tn), lambda i, j: (0, j)),
                  pl.BlockSpec((k, SMALL_W), lambda i, j: (0, 0))],
        out_specs=[pl.BlockSpec((tm, tn), lambda i, j: (i, j)),
                   pl.BlockSpec((tm, SMALL_W), lambda i, j: (i, 0))],
        out_shape=[_sds((t, n)), _sds((t, SMALL_W))],
        scratch_shapes=[pltpu.VMEM((tm, k), BF16)],
        compiler_params=_cparams("parallel", "arbitrary"),
        name="inproj",
    )(x2d, g, w_main, w_small)


def _normmm_body(x_ref, g_ref, w_ref, o_ref, xn_ref):
    @pl.when(pl.program_id(1) == 0)
    def _():
        xn_ref[...] = _rms(x_ref[...], g_ref[...]).astype(BF16)

    o_ref[...] = _mm(xn_ref[...], w_ref[...])


def _normmm(x2d, g, w, tm, tn):
    t, k = x2d.shape
    n = w.shape[1]
    return pl.pallas_call(
        _normmm_body,
        grid=(t // tm, n // tn),
        in_specs=[pl.BlockSpec((tm, k), lambda i, j: (i, 0)),
                  pl.BlockSpec((1, k), lambda i, j: (0, 0)),
                  pl.BlockSpec((k, tn), lambda i, j: (0, j))],
        out_specs=pl.BlockSpec((tm, tn), lambda i, j: (i, j)),
        out_shape=_sds((t, n)),
        scratch_shapes=[pltpu.VMEM((tm, k), BF16)],
        compiler_params=_cparams("parallel", "arbitrary"),
        name="memkv",
    )(x2d, g, w)


def _gla_body(*refs, ch, nc, has_state):
    if has_state:
        (q_ref, k_ref, v_ref, r_ref, sm_ref, wg_ref, bg_ref, ng_ref, tri_ref, s0_ref,
         o_ref, so_ref, st_scr, b_scr) = refs
    else:
        (q_ref, k_ref, v_ref, r_ref, sm_ref, wg_ref, bg_ref, ng_ref, tri_ref,
         o_ref, so_ref, st_scr, b_scr) = refs
    step = pl.program_id(1)

    @pl.when(step == 0)
    def _():
        for h in range(GLA_HEADS):
            if has_state:
                st_scr[h] = s0_ref[0, h].T
            else:
                st_scr[h] = jnp.zeros((GLA_DV, GLA_DK), F32)

    logit = _mm(sm_ref[:, 0:GLA_RANK], wg_ref[...], precision=HI) + bg_ref[...]
    g = (jnp.minimum(logit, 0.0) - jnp.log1p(jnp.exp(-jnp.abs(logit)))) * (1.0 / GLA_TAU)
    b_scr[...] = _mm(tri_ref[...], g, precision=HI)

    row_io = lax.broadcasted_iota(I32, (ch, GLA_DK), 0)
    lane_io = lax.broadcasted_iota(I32, (ch, ch), 1)

    def chunk(c, carry):
        rows = pl.ds(pl.multiple_of(c * ch, ch), ch)
        for h in range(GLA_HEADS):
            ksl = slice(h * GLA_DK, (h + 1) * GLA_DK)
            vsl = slice(h * GLA_DV, (h + 1) * GLA_DV)
            qh = q_ref[rows, ksl] * (GLA_DK ** -0.5)
            kh = k_ref[rows, ksl]
            vh = v_ref[rows, vsl]
            bh = b_scr[rows, ksl]
            st = st_scr[h]
            o = _nt(qh * jnp.exp(bh), st)
            att_t = jnp.zeros((ch, ch), F32)
            for i in range(ch):
                dec = jnp.exp(jnp.where(row_io <= i, bh[i:i + 1, :] - bh, NEG_INF))
                col = jnp.sum(kh * dec * qh[i:i + 1, :], axis=1, keepdims=True)
                att_t = jnp.where(lane_io == i, col, att_t)
            o = o + _tn(att_t, vh)
            b_last = bh[ch - 1:ch, :]
            st_scr[h] = jnp.exp(b_last) * st + _tn(vh, kh * jnp.exp(b_last - bh))
            rh = r_ref[rows, vsl]
            o_ref[rows, vsl] = _rms(o, ng_ref[...]) * _silu(rh)
        return carry

    lax.fori_loop(0, nc, chunk, 0)

    @pl.when(step == pl.num_programs(1) - 1)
    def _():
        for h in range(GLA_HEADS):
            so_ref[0, h] = st_scr[h].T


def _gla(proj, small, wts, s0, nb, seq, ch, lt):
    nsteps = seq // lt
    nc = lt // ch
    t = nb * seq
    rix = lambda b, s: b * nsteps + s
    idx = lax.broadcasted_iota(I32, (lt, lt), 0)
    jdx = lax.broadcasted_iota(I32, (lt, lt), 1)
    tri = ((jdx <= idx) & (idx // ch == jdx // ch)).astype(F32)
    in_specs = [pl.BlockSpec((lt, GLA_QK), lambda b, s: (rix(b, s), 0)),
                pl.BlockSpec((lt, GLA_QK), lambda b, s: (rix(b, s), 1)),
                pl.BlockSpec((lt, GLA_V), lambda b, s: (rix(b, s), 1)),
                pl.BlockSpec((lt, GLA_V), lambda b, s: (rix(b, s), 2)),
                pl.BlockSpec((lt, SMALL_W), lambda b, s: (rix(b, s), 0)),
                pl.BlockSpec((GLA_RANK, GLA_QK), lambda b, s: (0, 0)),
                pl.BlockSpec((1, GLA_QK), lambda b, s: (0, 0)),
                pl.BlockSpec((1, GLA_DV), lambda b, s: (0, 0)),
                pl.BlockSpec((lt, lt), lambda b, s: (0, 0))]
    args = [proj, proj, proj, proj, small, wts["w_gla_gate"], wts["b_gla_gate"], wts["gla_norm_g"], tri]
    if s0 is not None:
        in_specs.append(pl.BlockSpec((1, GLA_HEADS, GLA_DK, GLA_DV), lambda b, s: (b, 0, 0, 0)))
        args.append(s0)
    return pl.pallas_call(
        functools.partial(_gla_body, ch=ch, nc=nc, has_state=s0 is not None),
        grid=(nb, nsteps),
        in_specs=in_specs,
        out_specs=[pl.BlockSpec((lt, GLA_V), lambda b, s: (rix(b, s), 0)),
                   pl.BlockSpec((1, GLA_HEADS, GLA_DK, GLA_DV), lambda b, s: (b, 0, 0, 0))],
        out_shape=[_sds((t, GLA_V)), _sds((nb, GLA_HEADS, GLA_DK, GLA_DV))],
        scratch_shapes=[pltpu.VMEM((GLA_HEADS, GLA_DV, GLA_DK), F32), pltpu.VMEM((lt, GLA_QK), F32)],
        compiler_params=_cparams("parallel", "arbitrary"),
        name="gla",
    )(*args)


def _gdn_body(*refs, c, has_state, nsq):
    if has_state:
        (x_ref, z_ref, sm_ref, cw_ref, al_ref, dtb_ref, ng_ref, sel_ref, s0_ref, c0_ref,
         o_ref, so_ref, co_ref, st_scr, xp_scr) = refs
    else:
        (x_ref, z_ref, sm_ref, cw_ref, al_ref, dtb_ref, ng_ref, sel_ref,
         o_ref, so_ref, co_ref, st_scr, xp_scr) = refs
    step = pl.program_id(1)
    hist = GDN_CONV - 1
    base = SUBLANES - hist

    @pl.when(step == 0)
    def _():
        if has_state:
            st_scr[...] = s0_ref[0]
            xp_scr[base:SUBLANES, :] = c0_ref[0]
        else:
            st_scr[...] = jnp.zeros(st_scr.shape, F32)
            xp_scr[base:SUBLANES, :] = jnp.zeros((hist, GDN_CONV_CH), F32)

    xp_scr[SUBLANES:SUBLANES + c, :] = x_ref[...]
    conv = xp_scr[pl.ds(base, c), :] * cw_ref[0:1, :]
    for w in range(1, GDN_CONV):
        conv = conv + xp_scr[pl.ds(base + w, c), :] * cw_ref[w:w + 1, :]
    tail = xp_scr[pl.ds(c + base, hist), :]
    xp_scr[base:SUBLANES, :] = tail
    co_ref[0] = tail
    act = _silu(conv)

    sm = sm_ref[...]
    gdec = -jnp.exp(al_ref[...]) * _softplus(sm + dtb_ref[...])
    beta = jax.nn.sigmoid(sm)
    ii = lax.broadcasted_iota(I32, (c, c), 0)
    jj = lax.broadcasted_iota(I32, (c, c), 1)
    bcol = _mm((ii >= jj).astype(F32), gdec, precision=HI)
    brow = _nt(sel_ref[...], bcol, precision=HI)
    eye = (ii == jj).astype(F32)
    rb = (lambda x: x.astype(BF16)) if c >= 2 * SUBLANES else (lambda x: x.astype(BF16).astype(F32))

    for h in range(GDN_HEADS):
        sl = slice(h * GDN_DK, (h + 1) * GDN_DK)
        cq = act[:, h * GDN_DK:(h + 1) * GDN_DK]
        ck = act[:, GDN_QK + h * GDN_DK:GDN_QK + (h + 1) * GDN_DK]
        vh = act[:, 2 * GDN_QK + h * GDN_DV:2 * GDN_QK + (h + 1) * GDN_DV]
        qh = cq * lax.rsqrt(jnp.sum(cq * cq, axis=-1, keepdims=True) + EPS) * (GDN_DK ** -0.5)
        kh = ck * lax.rsqrt(jnp.sum(ck * ck, axis=-1, keepdims=True) + EPS)
        bc = bcol[:, DA_LANE + h:DA_LANE + h + 1]
        br = brow[h:h + 1, :]
        bt = beta[:, DB_LANE + h:DB_LANE + h + 1]
        gam = jnp.exp(jnp.where(ii >= jj, bc - br, NEG_INF))
        kq = rb(jnp.concatenate([kh, qh], axis=0))
        kb = kq[:c]
        kkqk = _nt(kq, kb)
        kk, qk = kkqk[:c], kkqk[c:]
        p = -(bt * jnp.where(ii > jj, gam, 0.0) * kk)
        ainv = eye + p
        pk = p
        for _ in range(nsq):
            pkb = rb(pk)
            pk = _mm(pkb, pkb)
            ainv = ainv + _mm(rb(ainv), rb(pk))
        s_h = st_scr[h]
        kqs = _mm(kq, rb(s_h))
        ks, qs = kqs[:c], kqs[c:]
        eb = jnp.exp(bc)
        u = rb(_mm(rb(ainv), rb(bt * (vh - eb * ks))))
        o = eb * qs + _mm(rb(qk * gam), u)
        b_last = bc[c - 1:c, :]
        st_scr[h] = jnp.exp(b_last) * s_h + _tn(rb(kh * jnp.exp(b_last - bc)), u)
        o_ref[:, sl] = _rms(o, ng_ref[...]) * _silu(z_ref[:, sl])

    @pl.when(step == pl.num_programs(1) - 1)
    def _():
        so_ref[0] = st_scr[...]


def _gdn(proj, small, wts, s0, conv0, nb, seq, c):
    nsteps = seq // c
    t = nb * seq
    rix = lambda b, s: b * nsteps + s
    nsq = max(c.bit_length() - 2, 0)
    in_specs = [pl.BlockSpec((c, GDN_CONV_CH), lambda b, s: (rix(b, s), 1)),
                pl.BlockSpec((c, GDN_V), lambda b, s: (rix(b, s), 6)),
                pl.BlockSpec((c, SMALL_W), lambda b, s: (rix(b, s), 0)),
                pl.BlockSpec((GDN_CONV, GDN_CONV_CH), lambda b, s: (0, 0)),
                pl.BlockSpec((1, SMALL_W), lambda b, s: (0, 0)),
                pl.BlockSpec((1, SMALL_W), lambda b, s: (0, 0)),
                pl.BlockSpec((1, GDN_DV), lambda b, s: (0, 0)),
                pl.BlockSpec((GDN_HEADS, SMALL_W), lambda b, s: (0, 0))]
    args = [proj, proj, small, wts["gdn_conv_w"], wts["a_log_lanes"], wts["dt_bias_lanes"], wts["gdn_norm_g"],
            wts["head_sel"]]
    if s0 is not None:
        in_specs += [pl.BlockSpec((1, GDN_HEADS, GDN_DK, GDN_DV), lambda b, s: (b, 0, 0, 0)),
                     pl.BlockSpec((1, GDN_CONV - 1, GDN_CONV_CH), lambda b, s: (b, 0, 0))]
        args += [s0, conv0]
    return pl.pallas_call(
        functools.partial(_gdn_body, c=c, has_state=s0 is not None, nsq=nsq),
        grid=(nb, nsteps),
        in_specs=in_specs,
        out_specs=[pl.BlockSpec((c, GDN_V), lambda b, s: (rix(b, s), 0)),
                   pl.BlockSpec((1, GDN_HEADS, GDN_DK, GDN_DV), lambda b, s: (b, 0, 0, 0)),
                   pl.BlockSpec((1, GDN_CONV - 1, GDN_CONV_CH), lambda b, s: (b, 0, 0))],
        out_shape=[_sds((t, GDN_V)), _sds((nb, GDN_HEADS, GDN_DK, GDN_DV)), _sds((nb, GDN_CONV - 1, GDN_CONV_CH))],
        scratch_shapes=[pltpu.VMEM((GDN_HEADS, GDN_DK, GDN_DV), F32),
                        pltpu.VMEM((c + SUBLANES, GDN_CONV_CH), F32)],
        compiler_params=_cparams("parallel", "arbitrary"),
        name="gdn",
    )(*args)


def _xattn_body(q_ref, k_ref, v_ref, o_ref):
    for h in range(MEM_HEADS):
        sl = slice(h * MEM_DH, (h + 1) * MEM_DH)
        sc = _nt(q_ref[:, sl], k_ref[0, :, sl]) * (MEM_DH ** -0.5)
        p = jnp.exp(sc - jnp.max(sc, axis=-1, keepdims=True))
        p = p / jnp.sum(p, axis=-1, keepdims=True)
        o_ref[:, sl] = _mm(p, v_ref[0, :, sl])


def _xattn(proj, k_arr, k_cb, v_arr, v_cb, nb, seq, lt):
    nsteps = seq // lt
    t = nb * seq
    return pl.pallas_call(
        _xattn_body,
        grid=(nb, nsteps),
        in_specs=[pl.BlockSpec((lt, MEM_W), lambda b, s: (b * nsteps + s, 7)),
                  pl.BlockSpec((1, N_MEM, MEM_W), lambda b, s: (b, 0, k_cb)),
                  pl.BlockSpec((1, N_MEM, MEM_W), lambda b, s: (b, 0, v_cb))],
        out_specs=pl.BlockSpec((lt, MEM_W), lambda b, s: (b * nsteps + s, 0)),
        out_shape=_sds((t, MEM_W)),
        compiler_params=_cparams("parallel", "arbitrary"),
        name="xattn",
    )(proj, k_arr, v_arr)


def _merge_body(oa_ref, ob_ref, oc_ref, ga_ref, gb_ref, gc_ref, ba_ref, bb_ref, bc_ref,
                wa_ref, wb_ref, wc_ref, m_ref):
    acc = jax.nn.sigmoid(ga_ref[...] + ba_ref[...]) * _mm(oa_ref[...].astype(BF16), wa_ref[...])
    acc = acc + jax.nn.sigmoid(gb_ref[...] + bb_ref[...]) * _mm(ob_ref[...].astype(BF16), wb_ref[...])
    acc = acc + jax.nn.sigmoid(gc_ref[...] + bc_ref[...]) * _mm(oc_ref[...].astype(BF16), wc_ref[...])
    m_ref[...] = acc.astype(BF16)


def _merge(o_a, o_b, o_c, proj, wts, tm):
    t = o_a.shape[0]
    row = lambda w: pl.BlockSpec((tm, w), lambda i: (i, 0))
    gate = lambda cb: pl.BlockSpec((tm, D_MODEL), lambda i: (i, cb))
    bias = lambda cb: pl.BlockSpec((1, D_MODEL), lambda i: (0, cb))
    wspec = lambda k: pl.BlockSpec((k, D_MODEL), lambda i: (0, 0))
    return pl.pallas_call(
        _merge_body,
        grid=(t // tm,),
        in_specs=[row(GLA_V), row(GDN_V), row(MEM_W), gate(4), gate(5), gate(6), bias(0), bias(1), bias(2),
                  wspec(GLA_V), wspec(GDN_V), wspec(MEM_W)],
        out_specs=pl.BlockSpec((tm, D_MODEL), lambda i: (i, 0)),
        out_shape=_sds((t, D_MODEL), BF16),
        compiler_params=_cparams("parallel"),
        name="merge",
    )(o_a, o_b, o_c, proj, proj, proj, wts["b_gates"], wts["b_gates"], wts["b_gates"],
      wts["w_br_gla"], wts["w_br_gdn"], wts["w_br_mem"])


def _outproj_body(m_ref, w_ref, x_ref, g_ref, x1_ref, h2_ref):
    x1 = x_ref[...] + _mm(m_ref[...], w_ref[...])
    x1_ref[...] = x1
    h2_ref[...] = _rms(x1, g_ref[...]).astype(BF16)


def _outproj(merged, w_out, x2d, g, tm):
    t = x2d.shape[0]
    blk = lambda: pl.BlockSpec((tm, D_MODEL), lambda i: (i, 0))
    return pl.pallas_call(
        _outproj_body,
        grid=(t // tm,),
        in_specs=[blk(), pl.BlockSpec((D_MODEL, D_MODEL), lambda i: (0, 0)), blk(),
                  pl.BlockSpec((1, D_MODEL), lambda i: (0, 0))],
        out_specs=[blk(), blk()],
        out_shape=[_sds((t, D_MODEL)), _sds((t, D_MODEL), BF16)],
        compiler_params=_cparams("parallel"),
        name="outproj",
    )(merged, w_out, x2d, g)


def _peerq_body(h_ref, w_ref, q_ref):
    q_ref[...] = _mm(h_ref[...], w_ref[...]).astype(BF16)


def _peerq(h2, wq, tm):
    t = h2.shape[0]
    n = wq.shape[1]
    return pl.pallas_call(
        _peerq_body,
        grid=(t // tm,),
        in_specs=[pl.BlockSpec((tm, D_MODEL), lambda i: (i, 0)), pl.BlockSpec((D_MODEL, n), lambda i: (0, 0))],
        out_specs=pl.BlockSpec((tm, n), lambda i: (i, 0)),
        out_shape=_sds((t, n), BF16),
        compiler_params=_cparams("parallel"),
        name="peerq",
    )(h2, wq)


def _top16(x, tv_ref, slot):
    io = lax.broadcasted_iota(I32, x.shape, 0)
    rank = jnp.full(x.shape, PEER_TOPK, I32)
    for r in range(PEER_TOPK):
        m = jnp.max(x, axis=0, keepdims=True)
        idx = jnp.min(jnp.where(x == m, io, x.shape[0]), axis=0, keepdims=True)
        hit = io == idx
        rank = jnp.where(hit, r, rank)
        x = jnp.where(hit, NEG_INF, x)
        tv_ref[slot, r:r + 1, :] = m
    return rank


def _peer_select(h, qp_ref, k1_ref, k2_ref, tv_scr, lh_scr, p1_scr, r2_scr, p2_scr, tl):
    base = pl.multiple_of(h * 2 * PEER_HALF, 2 * PEER_HALF)
    s1 = _nt(k1_ref[...], qp_ref[:, pl.ds(base, PEER_HALF)])
    s2 = _nt(k2_ref[...], qp_ref[:, pl.ds(base + PEER_HALF, PEER_HALF)])
    rank1 = _top16(s1, tv_scr, 0)
    rank2 = _top16(s2, tv_scr, 1)
    t1 = tv_scr[0]
    t2 = tv_scr[1]
    cand = jnp.concatenate([t1[a:a + 1, :] + t2 for a in range(PEER_TOPK)], axis=0)
    code = lax.broadcasted_iota(I32, cand.shape, 0)
    a_io = lax.broadcasted_iota(I32, (PEER_TOPK, tl), 0)
    limit = jnp.zeros((PEER_TOPK, tl), I32)
    for _ in range(PEER_TOPK):
        m = jnp.max(cand, axis=0, keepdims=True)
        idx = jnp.min(jnp.where(cand == m, code, cand.shape[0]), axis=0, keepdims=True)
        cand = jnp.where(code == idx, NEG_INF, cand)
        limit = limit + (a_io == (idx >> 4)).astype(I32)
    p1s = jnp.exp(t1 - t1[0:1, :])
    p2s = jnp.exp(t2 - t2[0:1, :])
    inner = jnp.zeros((PEER_TOPK, tl), F32)
    for b in range(PEER_TOPK):
        inner = inner + jnp.where(limit > b, p2s[b:b + 1, :], 0.0)
    inv_z = 1.0 / jnp.sum(p1s * inner, axis=0, keepdims=True)
    lim_e1 = jnp.zeros(rank1.shape, I32)
    for a in range(PEER_TOPK):
        lim_e1 = jnp.where(rank1 == a, limit[a:a + 1, :], lim_e1)
    lh_scr[h] = lim_e1.astype(F32)
    p1_scr[h] = jnp.where(rank1 < PEER_TOPK, jnp.exp(s1 - t1[0:1, :]), 0.0) * inv_z
    r2_scr[h] = rank2.astype(F32).astype(BF16)
    p2_scr[h] = jnp.exp(jnp.minimum(s2 - t2[0:1, :], 0.0)).astype(BF16)


def _peer_body(qp_ref, h2_ref, x1_ref, k1_ref, k2_ref, u_ref, vt_ref, fg_ref, y_ref,
               lh_scr, p1_scr, r2_scr, p2_scr, tv_scr, acc_scr, sca_scr, scb_scr, *, tl, te):
    e = pl.program_id(1)

    @pl.when(e == 0)
    def _():
        acc_scr[...] = jnp.zeros(acc_scr.shape, F32)

        def one_head(h, carry):
            _peer_select(h, qp_ref, k1_ref, k2_ref, tv_scr, lh_scr, p1_scr, r2_scr, p2_scr, tl)
            return carry

        lax.fori_loop(0, PEER_HEADS, one_head, 0)

    e1_per_sub = PEER_SUB // PEER_KEYS
    n_sub = te // PEER_SUB
    sc_bufs = (sca_scr, scb_scr)

    def pre_act(s):
        sc_bufs[s % 2][...] = _nt(u_ref[s * PEER_SUB:(s + 1) * PEER_SUB, :], h2_ref[...])

    pre_act(0)
    for s in range(n_sub):
        if s + 1 < n_sub:
            pre_act(s + 1)
        sc = sc_bufs[s % 2][...]
        act = (0.5 * sc * (1.0 + lax.erf(sc * 0.7071067811865476))).astype(BF16)
        parts = []
        for i in range(e1_per_sub):
            e1 = e * (te // PEER_KEYS) + s * e1_per_sub + i
            coef = jnp.zeros((PEER_KEYS, tl), BF16)
            for h in range(PEER_HEADS):
                lim = lh_scr[h, pl.ds(e1, 1), :].astype(BF16)
                p1 = p1_scr[h, pl.ds(e1, 1), :].astype(BF16)
                coef = coef + jnp.where(r2_scr[h] < lim, p2_scr[h], jnp.zeros((), BF16)) * p1
            parts.append(coef * act[i * PEER_KEYS:(i + 1) * PEER_KEYS, :])
        z = jnp.concatenate(parts, axis=0)
        acc_scr[...] += _mm(vt_ref[:, s * PEER_SUB:(s + 1) * PEER_SUB], z)

    @pl.when(e == pl.num_programs(1) - 1)
    def _():
        y_ref[...] = _rms(x1_ref[...] + acc_scr[...].T, fg_ref[...])


def _peer(qp, h2, x1, wts, tl, te):
    t = h2.shape[0]
    once = pl.Buffered(1)
    keys = lambda: pl.BlockSpec((PEER_KEYS, PEER_HALF), lambda i, e: (0, 0))
    sel = lambda dt=F32: pltpu.VMEM((PEER_HEADS, PEER_KEYS, tl), dt)
    return pl.pallas_call(
        functools.partial(_peer_body, tl=tl, te=te),
        grid=(t // tl, PEER_EXPERTS // te),
        in_specs=[pl.BlockSpec((tl, D_MODEL), lambda i, e: (i, 0), pipeline_mode=once),
                  pl.BlockSpec((tl, D_MODEL), lambda i, e: (i, 0)),
                  pl.BlockSpec((tl, D_MODEL), lambda i, e: (i, 0), pipeline_mode=once),
                  keys(), keys(),
                  pl.BlockSpec((te, D_MODEL), lambda i, e: (e, 0)),
                  pl.BlockSpec((D_MODEL, te), lambda i, e: (0, e)),
                  pl.BlockSpec((1, D_MODEL), lambda i, e: (0, 0))],
        out_specs=pl.BlockSpec((tl, D_MODEL), lambda i, e: (i, 0)),
        out_shape=_sds((t, D_MODEL)),
        scratch_shapes=[sel(), sel(), sel(BF16), sel(BF16), pltpu.VMEM((2, PEER_TOPK, tl), F32),
                        pltpu.VMEM((D_MODEL, tl), F32), pltpu.VMEM((PEER_SUB, tl), F32),
                        pltpu.VMEM((PEER_SUB, tl), F32)],
        compiler_params=_cparams("parallel", "arbitrary"),
        name="peer",
    )(qp, h2, x1, wts["peer_k1"], wts["peer_k2"], wts["peer_u"], wts["peer_vt"], wts["final_norm_g"])


def _pick(n, prefs):
    for p in prefs:
        if n % p == 0:
            return p
    return n


def _group(x3, k_arr, k_cb, v_arr, v_cb, s_gla0, s_gdn0, conv0, wts):
    nb, seq, d = x3.shape
    t = nb * seq
    x2d = x3.reshape(t, d)
    proj, small = _inproj(x2d, wts["norm_mix_g"], wts["w_main"], wts["w_small"],
                          tm=_pick(t, (1024, 512, 256, 128)), tn=1024)
    gla_ch = 16 if seq % 16 == 0 else SUBLANES
    gla_lt = _pick(seq, (256, 128, 64, 32, 16))
    o_a, s_gla = _gla(proj, small, wts, s_gla0, nb, seq, gla_ch, gla_lt)
    o_b, s_gdn, conv_new = _gdn(proj, small, wts, s_gdn0, conv0, nb, seq, _pick(seq, (64, 32, 16)))
    o_c = _xattn(proj, k_arr, k_cb, v_arr, v_cb, nb, seq, _pick(seq, (512, 256, 128)))
    merged = _merge(o_a, o_b, o_c, proj, wts, tm=_pick(t, (256, 128)))
    x1, h2 = _outproj(merged, wts["w_out"], x2d, wts["norm_ffn_g"], tm=_pick(t, (512, 256, 128)))
    qp = _peerq(h2, wts["peer_wq"], tm=_pick(t, (512, 256, 128)))
    y = _peer(qp, h2, x1, wts, tl=_pick(t, (512, 256, 128)), te=1024)
    return y.reshape(nb, seq, d), s_gla, s_gdn, conv_new


def _lanes(vec, start):
    return jnp.zeros((1, SMALL_W), F32).at[0, start:start + vec.shape[0]].set(vec.astype(F32))


def kernel(x_prompt, x_sample, mem_prompt, cache_mem_k, cache_mem_v, state_gla, state_gdn, state_conv, norm_mix_g, norm_mem_g, w_in, w_gla_gate, b_gla_gate, gla_norm_g, gdn_conv_w, gdn_a_log, gdn_dt_bias, gdn_norm_g, w_mem_kv, w_br_gla, w_br_gdn, w_br_mem, b_gates, w_out, norm_ffn_g, peer_wq, peer_k1, peer_k2, peer_u, peer_v, final_norm_g):
    depth = w_in.shape[0]
    assert depth == 1, "the chain below is written for a single layer"
    l = 0
    off, cols = 0, {}
    for name, n in zip(IN_NAMES, IN_SPLITS):
        cols[name] = w_in[l][:, off:off + n]
        off += n
    w_main = jnp.concatenate([cols[n] for n in MAIN_NAMES], axis=1).astype(BF16)
    pad = jnp.zeros((D_MODEL, SMALL_W - GLA_RANK - 2 * GDN_HEADS), F32)
    w_small = jnp.concatenate([cols["glr"], cols["da"], cols["db"], pad], axis=1).astype(BF16)
    wts = {
        "norm_mix_g": norm_mix_g[l][None], "w_main": w_main, "w_small": w_small,
        "w_gla_gate": w_gla_gate[l], "b_gla_gate": b_gla_gate[l][None], "gla_norm_g": gla_norm_g[l][None],
        "gdn_conv_w": gdn_conv_w[l], "a_log_lanes": _lanes(gdn_a_log[l], DA_LANE),
        "dt_bias_lanes": _lanes(gdn_dt_bias[l], DA_LANE), "gdn_norm_g": gdn_norm_g[l][None],
        "head_sel": (lax.broadcasted_iota(I32, (GDN_HEADS, SMALL_W), 1)
                     == lax.broadcasted_iota(I32, (GDN_HEADS, SMALL_W), 0) + DA_LANE).astype(F32),
        "w_br_gla": w_br_gla[l].astype(BF16), "w_br_gdn": w_br_gdn[l].astype(BF16),
        "w_br_mem": w_br_mem[l].astype(BF16), "b_gates": b_gates[l][None], "w_out": w_out[l].astype(BF16),
        "norm_ffn_g": norm_ffn_g[l][None], "peer_wq": peer_wq[l].astype(BF16),
        "peer_k1": peer_k1[l].astype(BF16), "peer_k2": peer_k2[l].astype(BF16),
        "peer_u": peer_u[l].astype(BF16), "peer_vt": peer_v[l].T.astype(BF16),
        "final_norm_g": final_norm_g[None],
    }
    nb_p, n_mem, _ = mem_prompt.shape
    nb_s = x_sample.shape[0]

    kv = _normmm(mem_prompt.reshape(nb_p * n_mem, D_MODEL), norm_mem_g[l][None], w_mem_kv[l].astype(BF16),
                 tm=_pick(nb_p * n_mem, (512, 256)), tn=1024)
    kv3 = kv.reshape(nb_p, n_mem, 2 * MEM_W)
    y_p, gla_p, gdn_p, conv_p = _group(x_prompt, kv3, 0, kv3, 1, None, None, None, wts)
    mk_p = kv3[:, :, :MEM_W].reshape(nb_p, n_mem, MEM_HEADS, MEM_DH)
    mv_p = kv3[:, :, MEM_W:].reshape(nb_p, n_mem, MEM_HEADS, MEM_DH)

    ck = cache_mem_k[l].reshape(nb_s, n_mem, MEM_W)
    cv = cache_mem_v[l].reshape(nb_s, n_mem, MEM_W)
    y_s, gla_s, gdn_s, conv_s = _group(x_sample, ck, 0, cv, 0, state_gla[l], state_gdn[l], state_conv[l], wts)

    return (y_p, y_s, gla_p[None], gdn_p[None], conv_p[None], mk_p[None], mv_p[None],
            gla_s[None], gdn_s[None], conv_s[None])
```

```python
import functools

import jax
import jax.numpy as jnp
from jax import lax
from jax.experimental import pallas as pl
from jax.experimental.pallas import tpu as pltpu

F32 = jnp.float32
BF16 = jnp.bfloat16
I32 = jnp.int32
HI = lax.Precision.HIGHEST
NEG_INF = float("-inf")

D_MODEL = 2048
EPS = 1e-6
N_MEM = 256
GLA_HEADS, GLA_DK, GLA_DV, GLA_RANK, GLA_TAU = 4, 128, 256, 16, 16.0
GLA_QK, GLA_V = GLA_HEADS * GLA_DK, GLA_HEADS * GLA_DV
GDN_HEADS, GDN_DK, GDN_DV, GDN_CONV = 8, 128, 128, 4
GDN_QK, GDN_V = GDN_HEADS * GDN_DK, GDN_HEADS * GDN_DV
GDN_CONV_CH = 2 * GDN_QK + GDN_V
MEM_HEADS, MEM_DH = 4, 256
MEM_W = MEM_HEADS * MEM_DH
PEER_KEYS, PEER_HEADS, PEER_HALF, PEER_TOPK = 128, 8, 128, 16
PEER_EXPERTS = PEER_KEYS * PEER_KEYS
PEER_SUB = 256
N_GATES = 3 * D_MODEL
IN_SPLITS = (GLA_QK, GLA_QK, GLA_V, GLA_V, GLA_RANK, GDN_CONV_CH, GDN_V, GDN_HEADS, GDN_HEADS, MEM_W, N_GATES)
IN_NAMES = ("gq", "gk", "gv", "gr", "glr", "dqkv", "dz", "da", "db", "mq", "gates")
MAIN_NAMES = ("gq", "gk", "gv", "gr", "dqkv", "dz", "mq", "gates")
N_MAIN = 2 * GLA_QK + 2 * GLA_V + GDN_CONV_CH + GDN_V + MEM_W + N_GATES
SMALL_W = 128
DA_LANE, DB_LANE = GLA_RANK, GLA_RANK + GDN_HEADS

VMEM_LIMIT_BYTES = 56 * 1024 * 1024
SUBLANES, LANES = 8, 128
MXU_DIM = 256


def _cparams(*sem):
    return pltpu.CompilerParams(dimension_semantics=sem, vmem_limit_bytes=VMEM_LIMIT_BYTES)


def _sds(shape, dtype=F32):
    return jax.ShapeDtypeStruct(shape, dtype)


def _mm(a, b, **kw):
    return jnp.dot(a, b, preferred_element_type=F32, **kw)


def _nt(a, b, **kw):
    return lax.dot_general(a, b, (((1,), (1,)), ((), ())), preferred_element_type=F32, **kw)


def _tn(a, b):
    return lax.dot_general(a, b, (((0,), (0,)), ((), ())), preferred_element_type=F32)


def _softplus(x):
    return jnp.maximum(x, 0.0) + jnp.log1p(jnp.exp(-jnp.abs(x)))


def _silu(x):
    return x * jax.nn.sigmoid(x)


def _rms(x, g):
    return x * lax.rsqrt(jnp.mean(x * x, axis=-1, keepdims=True) + EPS) * g


def _inproj_body(x_ref, g_ref, w_ref, ws_ref, o_ref, os_ref, xn_ref):
    @pl.when(pl.program_id(1) == 0)
    def _():
        xn = _rms(x_ref[...], g_ref[...]).astype(BF16)
        xn_ref[...] = xn
        os_ref[...] = _mm(xn, ws_ref[...])

    o_ref[...] = _mm(xn_ref[...], w_ref[...])


def _inproj(x2d, g, w_main, w_small, tm, tn):
    t, k = x2d.shape
    n = w_main.shape[1]
    return pl.pallas_call(
        _inproj_body,
        grid=(t // tm, n // tn),
        in_specs=[pl.BlockSpec((tm, k), lambda i, j: (i, 0)),
                  pl.BlockSpec((1, k), lambda i, j: (0, 0)),
                  pl.BlockSpec((k, tn), lambda i, j: (0, j)),
                  pl.BlockSpec((k, SMALL_W), lambda i, j: (0, 0))],
        out_specs=[pl.BlockSpec((tm, tn), lambda i, j: (i, j)),
                   pl.BlockSpec((tm, SMALL_W), lambda i, j: (i, 0))],
        out_shape=[_sds((t, n)), _sds((t, SMALL_W))],
        scratch_shapes=[pltpu.VMEM((tm, k), BF16)],
        compiler_params=_cparams("parallel", "arbitrary"),
        name="inproj",
    )(x2d, g, w_main, w_small)


def _normmm_body(x_ref, g_ref, w_ref, o_ref, xn_ref):
    @pl.when(pl.program_id(1) == 0)
    def _():
        xn_ref[...] = _rms(x_ref[...], g_ref[...]).astype(BF16)

    o_ref[...] = _mm(xn_ref[...], w_ref[...])


def _normmm(x2d, g, w, tm, tn):
    t, k = x2d.shape
    n = w.shape[1]
    return pl.pallas_call(
        _normmm_body,
        grid=(t // tm, n // tn),
        in_specs=[pl.BlockSpec((tm, k), lambda i, j: (i, 0)),
                  pl.BlockSpec((1, k), lambda i, j: (0, 0)),
                  pl.BlockSpec((k, tn), lambda i, j: (0, j))],
        out_specs=pl.BlockSpec((tm, tn), lambda i, j: (i, j)),
        out_shape=_sds((t, n)),
        scratch_shapes=[pltpu.VMEM((tm, k), BF16)],
        compiler_params=_cparams("parallel", "arbitrary"),
        name="memkv",
    )(x2d, g, w)


def _gla_body(*refs, ch, nc, has_state):
    if has_state:
        (q_ref, k_ref, v_ref, r_ref, sm_ref, wg_ref, bg_ref, ng_ref, tri_ref, s0_ref,
         o_ref, so_ref, st_scr, b_scr) = refs
    else:
        (q_ref, k_ref, v_ref, r_ref, sm_ref, wg_ref, bg_ref, ng_ref, tri_ref,
         o_ref, so_ref, st_scr, b_scr) = refs
    step = pl.program_id(1)

    @pl.when(step == 0)
    def _():
        for h in range(GLA_HEADS):
            if has_state:
                st_scr[h] = s0_ref[0, h].T
            else:
                st_scr[h] = jnp.zeros((GLA_DV, GLA_DK), F32)

    logit = _mm(sm_ref[:, 0:GLA_RANK], wg_ref[...], precision=HI) + bg_ref[...]
    g = (jnp.minimum(logit, 0.0) - jnp.log1p(jnp.exp(-jnp.abs(logit)))) * (1.0 / GLA_TAU)
    b_scr[...] = _mm(tri_ref[...], g, precision=HI)

    row_io = lax.broadcasted_iota(I32, (ch, GLA_DK), 0)
    lane_io = lax.broadcasted_iota(I32, (ch, ch), 1)

    def chunk(c, carry):
        rows = pl.ds(pl.multiple_of(c * ch, ch), ch)
        for h in range(GLA_HEADS):
            ksl = slice(h * GLA_DK, (h + 1) * GLA_DK)
            vsl = slice(h * GLA_DV, (h + 1) * GLA_DV)
            qh = q_ref[rows, ksl] * (GLA_DK ** -0.5)
            kh = k_ref[rows, ksl]
            vh = v_ref[rows, vsl]
            bh = b_scr[rows, ksl]
            st = st_scr[h]
            o = _nt(qh * jnp.exp(bh), st)
            att_t = jnp.zeros((ch, ch), F32)
            for i in range(ch):
                dec = jnp.exp(jnp.where(row_io <= i, bh[i:i + 1, :] - bh, NEG_INF))
                col = jnp.sum(kh * dec * qh[i:i + 1, :], axis=1, keepdims=True)
                att_t = jnp.where(lane_io == i, col, att_t)
            o = o + _tn(att_t, vh)
            b_last = bh[ch - 1:ch, :]
            st_scr[h] = jnp.exp(b_last) * st + _tn(vh, kh * jnp.exp(b_last - bh))
            rh = r_ref[rows, vsl]
            o_ref[rows, vsl] = _rms(o, ng_ref[...]) * _silu(rh)
        return carry

    lax.fori_loop(0, nc, chunk, 0)

    @pl.when(step == pl.num_programs(1) - 1)
    def _():
        for h in range(GLA_HEADS):
            so_ref[0, h] = st_scr[h].T


def _gla(proj, small, wts, s0, nb, seq, ch, lt):
    nsteps = seq // lt
    nc = lt // ch
    t = nb * seq
    rix = lambda b, s: b * nsteps + s
    idx = lax.broadcasted_iota(I32, (lt, lt), 0)
    jdx = lax.broadcasted_iota(I32, (lt, lt), 1)
    tri = ((jdx <= idx) & (idx // ch == jdx // ch)).astype(F32)
    in_specs = [pl.BlockSpec((lt, GLA_QK), lambda b, s: (rix(b, s), 0)),
                pl.BlockSpec((lt, GLA_QK), lambda b, s: (rix(b, s), 1)),
                pl.BlockSpec((lt, GLA_V), lambda b, s: (rix(b, s), 1)),
                pl.BlockSpec((lt, GLA_V), lambda b, s: (rix(b, s), 2)),
                pl.BlockSpec((lt, SMALL_W), lambda b, s: (rix(b, s), 0)),
                pl.BlockSpec((GLA_RANK, GLA_QK), lambda b, s: (0, 0)),
                pl.BlockSpec((1, GLA_QK), lambda b, s: (0, 0)),
                pl.BlockSpec((1, GLA_DV), lambda b, s: (0, 0)),
                pl.BlockSpec((lt, lt), lambda b, s: (0, 0))]
    args = [proj, proj, proj, proj, small, wts["w_gla_gate"], wts["b_gla_gate"], wts["gla_norm_g"], tri]
    if s0 is not None:
        in_specs.append(pl.BlockSpec((1, GLA_HEADS, GLA_DK, GLA_DV), lambda b, s: (b, 0, 0, 0)))
        args.append(s0)
    return pl.pallas_call(
        functools.partial(_gla_body, ch=ch, nc=nc, has_state=s0 is not None),
        grid=(nb, nsteps),
        in_specs=in_specs,
        out_specs=[pl.BlockSpec((lt, GLA_V), lambda b, s: (rix(b, s), 0)),
                   pl.BlockSpec((1, GLA_HEADS, GLA_DK, GLA_DV), lambda b, s: (b, 0, 0, 0))],
        out_shape=[_sds((t, GLA_V)), _sds((nb, GLA_HEADS, GLA_DK, GLA_DV))],
        scratch_shapes=[pltpu.VMEM((GLA_HEADS, GLA_DV, GLA_DK), F32), pltpu.VMEM((lt, GLA_QK), F32)],
        compiler_params=_cparams("parallel", "arbitrary"),
        name="gla",
    )(*args)


def _gdn_body(*refs, c, has_state, nsq, grp):
    if has_state:
        (x_ref, z_ref, sm_ref, cw_ref, al_ref, dtb_ref, ng_ref, s0_ref, c0_ref,
         o_ref, so_ref, co_ref, st_scr, xp_scr) = refs
    else:
        (x_ref, z_ref, sm_ref, cw_ref, al_ref, dtb_ref, ng_ref,
         o_ref, so_ref, co_ref, st_scr, xp_scr) = refs
    step = pl.program_id(1)
    hist = GDN_CONV - 1
    base = SUBLANES - hist

    @pl.when(step == 0)
    def _():
        if has_state:
            st_scr[...] = s0_ref[0]
            xp_scr[base:SUBLANES, :] = c0_ref[0]
        else:
            st_scr[...] = jnp.zeros(st_scr.shape, F32)
            xp_scr[base:SUBLANES, :] = jnp.zeros((hist, GDN_CONV_CH), F32)

    xp_scr[SUBLANES:SUBLANES + c, :] = x_ref[...]
    conv = xp_scr[pl.ds(base, c), :] * cw_ref[0:1, :]
    for w in range(1, GDN_CONV):
        conv = conv + xp_scr[pl.ds(base + w, c), :] * cw_ref[w:w + 1, :]
    tail = xp_scr[pl.ds(c + base, hist), :]
    xp_scr[base:SUBLANES, :] = tail
    co_ref[0] = tail
    act = _silu(conv)

    sm = sm_ref[...]
    gdec = -jnp.exp(al_ref[...]) * _softplus(sm + dtb_ref[...])
    beta = jax.nn.sigmoid(sm)
    ci = lax.broadcasted_iota(I32, (c, c), 0)
    cj = lax.broadcasted_iota(I32, (c, c), 1)
    bcol = _mm((ci >= cj).astype(F32), gdec, precision=HI)
    rb = (lambda x: x.astype(BF16)) if c >= 2 * SUBLANES else (lambda x: x.astype(BF16).astype(F32))

    rows = grp * c
    shift = c.bit_length() - 1
    ii = lax.broadcasted_iota(I32, (rows, rows), 0)
    jj = lax.broadcasted_iota(I32, (rows, rows), 1)
    same = (ii >> shift) == (jj >> shift)
    incl = same & (ii >= jj)
    strict = same & (ii > jj)
    eye = (ii == jj).astype(F32)
    first = ((lax.broadcasted_iota(I32, (SUBLANES, LANES), 0) == 0)
             & (lax.broadcasted_iota(I32, (SUBLANES, LANES), 1) == 0)).astype(F32)
    stack = lambda parts: jnp.concatenate(parts, axis=0) if len(parts) > 1 else parts[0]

    for g in range(GDN_HEADS // grp):
        heads = list(range(g * grp, (g + 1) * grp))
        q_parts, k_parts, v_parts = [], [], []
        for h in heads:
            cq = act[:, h * GDN_DK:(h + 1) * GDN_DK]
            ck = act[:, GDN_QK + h * GDN_DK:GDN_QK + (h + 1) * GDN_DK]
            q_parts.append(cq * lax.rsqrt(jnp.sum(cq * cq, axis=-1, keepdims=True) + EPS) * (GDN_DK ** -0.5))
            k_parts.append(ck * lax.rsqrt(jnp.sum(ck * ck, axis=-1, keepdims=True) + EPS))
            v_parts.append(act[:, 2 * GDN_QK + h * GDN_DV:2 * GDN_QK + (h + 1) * GDN_DV])
        k_all, v_all = stack(k_parts), stack(v_parts)
        bc = stack([bcol[:, DA_LANE + h:DA_LANE + h + 1] for h in heads])
        bt = stack([beta[:, DB_LANE + h:DB_LANE + h + 1] for h in heads])
        br = _nt(first, jnp.broadcast_to(bc, (rows, LANES)), precision=HI)[0:1, :]
        gam = jnp.exp(jnp.where(incl, bc - br, NEG_INF))
        kq = rb(stack(k_parts + q_parts))
        kkqk = _nt(kq, kq[:rows])
        kk, qk = kkqk[:rows], kkqk[rows:]
        p = -(bt * jnp.where(strict, gam, 0.0) * kk)
        ainv = eye + p
        pk = p
        for _ in range(nsq):
            pkb = rb(pk)
            pk = _mm(pkb, pkb)
            ainv = ainv + _mm(rb(ainv), rb(pk))
        ks_parts, qs_parts = [], []
        for n, h in enumerate(heads):
            kq_h = stack([kq[n * c:(n + 1) * c], kq[rows + n * c:rows + (n + 1) * c]])
            kqs = _mm(kq_h, rb(st_scr[h]))
            ks_parts.append(kqs[:c])
            qs_parts.append(kqs[c:])
        eb = jnp.exp(bc)
        u = rb(_mm(rb(ainv), rb(bt * (v_all - eb * stack(ks_parts)))))
        o = eb * stack(qs_parts) + _mm(rb(qk * gam), u)
        for n, h in enumerate(heads):
            r0, r1 = n * c, (n + 1) * c
            sl = slice(h * GDN_DK, (h + 1) * GDN_DK)
            b_last = bc[r1 - 1:r1, :]
            st_scr[h] = jnp.exp(b_last) * st_scr[h] + _tn(rb(k_all[r0:r1] * jnp.exp(b_last - bc[r0:r1])), u[r0:r1])
            o_ref[:, sl] = _rms(o[r0:r1], ng_ref[...]) * _silu(z_ref[:, sl])

    @pl.when(step == pl.num_programs(1) - 1)
    def _():
        so_ref[0] = st_scr[...]


def _gdn(proj, small, wts, s0, conv0, nb, seq, c):
    nsteps = seq // c
    t = nb * seq
    rix = lambda b, s: b * nsteps + s
    nsq = max(c.bit_length() - 2, 0)
    grp = min(GDN_HEADS, MXU_DIM // c)
    in_specs = [pl.BlockSpec((c, GDN_CONV_CH), lambda b, s: (rix(b, s), 1)),
                pl.BlockSpec((c, GDN_V), lambda b, s: (rix(b, s), 6)),
                pl.BlockSpec((c, SMALL_W), lambda b, s: (rix(b, s), 0)),
                pl.BlockSpec((GDN_CONV, GDN_CONV_CH), lambda b, s: (0, 0)),
                pl.BlockSpec((1, SMALL_W), lambda b, s: (0, 0)),
                pl.BlockSpec((1, SMALL_W), lambda b, s: (0, 0)),
                pl.BlockSpec((1, GDN_DV), lambda b, s: (0, 0))]
    args = [proj, proj, small, wts["gdn_conv_w"], wts["a_log_lanes"], wts["dt_bias_lanes"], wts["gdn_norm_g"]]
    if s0 is not None:
        in_specs += [pl.BlockSpec((1, GDN_HEADS, GDN_DK, GDN_DV), lambda b, s: (b, 0, 0, 0)),
                     pl.BlockSpec((1, GDN_CONV - 1, GDN_CONV_CH), lambda b, s: (b, 0, 0))]
        args += [s0, conv0]
    return pl.pallas_call(
        functools.partial(_gdn_body, c=c, has_state=s0 is not None, nsq=nsq, grp=grp),
        grid=(nb, nsteps),
        in_specs=in_specs,
        out_specs=[pl.BlockSpec((c, GDN_V), lambda b, s: (rix(b, s), 0)),
                   pl.BlockSpec((1, GDN_HEADS, GDN_DK, GDN_DV), lambda b, s: (b, 0, 0, 0)),
                   pl.BlockSpec((1, GDN_CONV - 1, GDN_CONV_CH), lambda b, s: (b, 0, 0))],
        out_shape=[_sds((t, GDN_V)), _sds((nb, GDN_HEADS, GDN_DK, GDN_DV)), _sds((nb, GDN_CONV - 1, GDN_CONV_CH))],
        scratch_shapes=[pltpu.VMEM((GDN_HEADS, GDN_DK, GDN_DV), F32),
                        pltpu.VMEM((c + SUBLANES, GDN_CONV_CH), F32)],
        compiler_params=_cparams("parallel", "arbitrary"),
        name="gdn",
    )(*args)


def _xattn_body(q_ref, k_ref, v_ref, o_ref, *, per_head_kv):
    for h in range(MEM_HEADS):
        sl = slice(h * MEM_DH, (h + 1) * MEM_DH)
        if per_head_kv:
            kh, vh = k_ref[0, :, h, :], v_ref[0, :, h, :]
        else:
            kh, vh = k_ref[0, :, sl], v_ref[0, :, sl]
        sc = _nt(q_ref[:, sl], kh) * (MEM_DH ** -0.5)
        p = jnp.exp(sc - jnp.max(sc, axis=-1, keepdims=True))
        p = p / jnp.sum(p, axis=-1, keepdims=True)
        o_ref[:, sl] = _mm(p, vh)


def _xattn(proj, k_arr, k_cb, v_arr, v_cb, nb, seq, lt):
    nsteps = seq // lt
    t = nb * seq
    per_head_kv = k_arr.ndim == 4
    if per_head_kv:
        kv_spec = lambda cb: pl.BlockSpec((1, N_MEM, MEM_HEADS, MEM_DH), lambda b, s: (b, 0, 0, 0))
    else:
        kv_spec = lambda cb: pl.BlockSpec((1, N_MEM, MEM_W), lambda b, s: (b, 0, cb))
    return pl.pallas_call(
        functools.partial(_xattn_body, per_head_kv=per_head_kv),
        grid=(nb, nsteps),
        in_specs=[pl.BlockSpec((lt, MEM_W), lambda b, s: (b * nsteps + s, 7)),
                  kv_spec(k_cb), kv_spec(v_cb)],
        out_specs=pl.BlockSpec((lt, MEM_W), lambda b, s: (b * nsteps + s, 0)),
        out_shape=_sds((t, MEM_W)),
        compiler_params=_cparams("parallel", "arbitrary"),
        name="xattn",
    )(proj, k_arr, v_arr)


def _merge_body(oa_ref, ob_ref, oc_ref, ga_ref, gb_ref, gc_ref, ba_ref, bb_ref, bc_ref,
                wa_ref, wb_ref, wc_ref, m_ref):
    acc = jax.nn.sigmoid(ga_ref[...] + ba_ref[...]) * _mm(oa_ref[...].astype(BF16), wa_ref[...])
    acc = acc + jax.nn.sigmoid(gb_ref[...] + bb_ref[...]) * _mm(ob_ref[...].astype(BF16), wb_ref[...])
    acc = acc + jax.nn.sigmoid(gc_ref[...] + bc_ref[...]) * _mm(oc_ref[...].astype(BF16), wc_ref[...])
    m_ref[...] = acc.astype(BF16)


def _merge(o_a, o_b, o_c, proj, wts, tm):
    t = o_a.shape[0]
    row = lambda w: pl.BlockSpec((tm, w), lambda i: (i, 0))
    gate = lambda cb: pl.BlockSpec((tm, D_MODEL), lambda i: (i, cb))
    bias = lambda cb: pl.BlockSpec((1, D_MODEL), lambda i: (0, cb))
    wspec = lambda k: pl.BlockSpec((k, D_MODEL), lambda i: (0, 0))
    return pl.pallas_call(
        _merge_body,
        grid=(t // tm,),
        in_specs=[row(GLA_V), row(GDN_V), row(MEM_W), gate(4), gate(5), gate(6), bias(0), bias(1), bias(2),
                  wspec(GLA_V), wspec(GDN_V), wspec(MEM_W)],
        out_specs=pl.BlockSpec((tm, D_MODEL), lambda i: (i, 0)),
        out_shape=_sds((t, D_MODEL), BF16),
        compiler_params=_cparams("parallel"),
        name="merge",
    )(o_a, o_b, o_c, proj, proj, proj, wts["b_gates"], wts["b_gates"], wts["b_gates"],
      wts["w_br_gla"], wts["w_br_gdn"], wts["w_br_mem"])


def _outproj_body(m_ref, w_ref, x_ref, g_ref, x1_ref, h2_ref):
    x1 = x_ref[...] + _mm(m_ref[...], w_ref[...])
    x1_ref[...] = x1
    h2_ref[...] = _rms(x1, g_ref[...]).astype(BF16)


def _outproj(merged, w_out, x2d, g, tm):
    t = x2d.shape[0]
    blk = lambda: pl.BlockSpec((tm, D_MODEL), lambda i: (i, 0))
    return pl.pallas_call(
        _outproj_body,
        grid=(t // tm,),
        in_specs=[blk(), pl.BlockSpec((D_MODEL, D_MODEL), lambda i: (0, 0)), blk(),
                  pl.BlockSpec((1, D_MODEL), lambda i: (0, 0))],
        out_specs=[blk(), blk()],
        out_shape=[_sds((t, D_MODEL)), _sds((t, D_MODEL), BF16)],
        compiler_params=_cparams("parallel"),
        name="outproj",
    )(merged, w_out, x2d, g)


def _peerq_body(h_ref, w_ref, q_ref):
    q_ref[...] = _mm(h_ref[...], w_ref[...]).astype(BF16)


def _peerq(h2, wq, tm):
    t = h2.shape[0]
    n = wq.shape[1]
    return pl.pallas_call(
        _peerq_body,
        grid=(t // tm,),
        in_specs=[pl.BlockSpec((tm, D_MODEL), lambda i: (i, 0)), pl.BlockSpec((D_MODEL, n), lambda i: (0, 0))],
        out_specs=pl.BlockSpec((tm, n), lambda i: (i, 0)),
        out_shape=_sds((t, n), BF16),
        compiler_params=_cparams("parallel"),
        name="peerq",
    )(h2, wq)


def _top16(x, tv_ref, slot):
    io = lax.broadcasted_iota(I32, x.shape, 0)
    rank = jnp.full(x.shape, PEER_TOPK, I32)
    for r in range(PEER_TOPK):
        m = jnp.max(x, axis=0, keepdims=True)
        idx = jnp.min(jnp.where(x == m, io, x.shape[0]), axis=0, keepdims=True)
        hit = io == idx
        rank = jnp.where(hit, r, rank)
        x = jnp.where(hit, NEG_INF, x)
        tv_ref[slot, r:r + 1, :] = m
    return rank


def _peer_select(h, qp_ref, k1_ref, k2_ref, tv_scr, lh_scr, p1_scr, r2_scr, p2_scr, tl):
    base = pl.multiple_of(h * 2 * PEER_HALF, 2 * PEER_HALF)
    s1 = _nt(k1_ref[...], qp_ref[:, pl.ds(base, PEER_HALF)])
    s2 = _nt(k2_ref[...], qp_ref[:, pl.ds(base + PEER_HALF, PEER_HALF)])
    rank1 = _top16(s1, tv_scr, 0)
    rank2 = _top16(s2, tv_scr, 1)
    t1 = tv_scr[0]
    t2 = tv_scr[1]
    cand = jnp.concatenate([t1[a:a + 1, :] + t2 for a in range(PEER_TOPK)], axis=0)
    code = lax.broadcasted_iota(I32, cand.shape, 0)
    a_io = lax.broadcasted_iota(I32, (PEER_TOPK, tl), 0)
    limit = jnp.zeros((PEER_TOPK, tl), I32)
    for _ in range(PEER_TOPK):
        m = jnp.max(cand, axis=0, keepdims=True)
        idx = jnp.min(jnp.where(cand == m, code, cand.shape[0]), axis=0, keepdims=True)
        cand = jnp.where(code == idx, NEG_INF, cand)
        limit = limit + (a_io == (idx >> 4)).astype(I32)
    p1s = jnp.exp(t1 - t1[0:1, :])
    p2s = jnp.exp(t2 - t2[0:1, :])
    inner = jnp.zeros((PEER_TOPK, tl), F32)
    for b in range(PEER_TOPK):
        inner = inner + jnp.where(limit > b, p2s[b:b + 1, :], 0.0)
    inv_z = 1.0 / jnp.sum(p1s * inner, axis=0, keepdims=True)
    lim_e1 = jnp.zeros(rank1.shape, I32)
    for a in range(PEER_TOPK):
        lim_e1 = jnp.where(rank1 == a, limit[a:a + 1, :], lim_e1)
    lh_scr[h] = lim_e1.astype(F32)
    p1_scr[h] = jnp.where(rank1 < PEER_TOPK, jnp.exp(s1 - t1[0:1, :]), 0.0) * inv_z
    r2_scr[h] = rank2.astype(F32).astype(BF16)
    p2_scr[h] = jnp.exp(jnp.minimum(s2 - t2[0:1, :], 0.0)).astype(BF16)


def _peer_body(qp_ref, h2_ref, x1_ref, k1_ref, k2_ref, u_ref, vt_ref, fg_ref, y_ref,
               lh_scr, p1_scr, r2_scr, p2_scr, tv_scr, acc_scr, sca_scr, scb_scr, *, tl, te):
    e = pl.program_id(1)

    @pl.when(e == 0)
    def _():
        acc_scr[...] = jnp.zeros(acc_scr.shape, F32)

        def one_head(h, carry):
            _peer_select(h, qp_ref, k1_ref, k2_ref, tv_scr, lh_scr, p1_scr, r2_scr, p2_scr, tl)
            return carry

        lax.fori_loop(0, PEER_HEADS, one_head, 0)

    e1_per_sub = PEER_SUB // PEER_KEYS
    n_sub = te // PEER_SUB
    sc_bufs = (sca_scr, scb_scr)

    def pre_act(s):
        sc_bufs[s % 2][...] = _nt(u_ref[s * PEER_SUB:(s + 1) * PEER_SUB, :], h2_ref[...])

    pre_act(0)
    for s in range(n_sub):
        if s + 1 < n_sub:
            pre_act(s + 1)
        sc = sc_bufs[s % 2][...]
        act = (0.5 * sc * (1.0 + lax.erf(sc * 0.7071067811865476))).astype(BF16)
        parts = []
        for i in range(e1_per_sub):
            e1 = e * (te // PEER_KEYS) + s * e1_per_sub + i
            coef = jnp.zeros((PEER_KEYS, tl), BF16)
            for h in range(PEER_HEADS):
                lim = lh_scr[h, pl.ds(e1, 1), :].astype(BF16)
                p1 = p1_scr[h, pl.ds(e1, 1), :].astype(BF16)
                coef = coef + jnp.where(r2_scr[h] < lim, p2_scr[h], jnp.zeros((), BF16)) * p1
            parts.append(coef * act[i * PEER_KEYS:(i + 1) * PEER_KEYS, :])
        z = jnp.concatenate(parts, axis=0)
        acc_scr[...] += _mm(vt_ref[:, s * PEER_SUB:(s + 1) * PEER_SUB], z)

    @pl.when(e == pl.num_programs(1) - 1)
    def _():
        y_ref[...] = _rms(x1_ref[...] + acc_scr[...].T, fg_ref[...])


def _peer(qp, h2, x1, wts, tl, te):
    t = h2.shape[0]
    once = pl.Buffered(1)
    keys = lambda: pl.BlockSpec((PEER_KEYS, PEER_HALF), lambda i, e: (0, 0))
    sel = lambda dt=F32: pltpu.VMEM((PEER_HEADS, PEER_KEYS, tl), dt)
    return pl.pallas_call(
        functools.partial(_peer_body, tl=tl, te=te),
        grid=(t // tl, PEER_EXPERTS // te),
        in_specs=[pl.BlockSpec((tl, D_MODEL), lambda i, e: (i, 0), pipeline_mode=once),
                  pl.BlockSpec((tl, D_MODEL), lambda i, e: (i, 0)),
                  pl.BlockSpec((tl, D_MODEL), lambda i, e: (i, 0), pipeline_mode=once),
                  keys(), keys(),
                  pl.BlockSpec((te, D_MODEL), lambda i, e: (e, 0)),
                  pl.BlockSpec((D_MODEL, te), lambda i, e: (0, e)),
                  pl.BlockSpec((1, D_MODEL), lambda i, e: (0, 0))],
        out_specs=pl.BlockSpec((tl, D_MODEL), lambda i, e: (i, 0)),
        out_shape=_sds((t, D_MODEL)),
        scratch_shapes=[sel(), sel(), sel(BF16), sel(BF16), pltpu.VMEM((2, PEER_TOPK, tl), F32),
                        pltpu.VMEM((D_MODEL, tl), F32), pltpu.VMEM((PEER_SUB, tl), F32),
                        pltpu.VMEM((PEER_SUB, tl), F32)],
        compiler_params=_cparams("parallel", "arbitrary"),
        name="peer",
    )(qp, h2, x1, wts["peer_k1"], wts["peer_k2"], wts["peer_u"], wts["peer_vt"], wts["final_norm_g"])


def _pick(n, prefs):
    for p in prefs:
        if n % p == 0:
            return p
    return n


def _group(x3, k_arr, k_cb, v_arr, v_cb, s_gla0, s_gdn0, conv0, wts):
    nb, seq, d = x3.shape
    t = nb * seq
    x2d = x3.reshape(t, d)
    proj, small = _inproj(x2d, wts["norm_mix_g"], wts["w_main"], wts["w_small"],
                          tm=_pick(t, (1024, 512, 256, 128)), tn=1024)
    gla_ch = 16 if seq % 16 == 0 else SUBLANES
    gla_lt = _pick(seq, (256, 128, 64, 32, 16))
    o_a, s_gla = _gla(proj, small, wts, s_gla0, nb, seq, gla_ch, gla_lt)
    o_b, s_gdn, conv_new = _gdn(proj, small, wts, s_gdn0, conv0, nb, seq, _pick(seq, (64, 32, 16)))
    o_c = _xattn(proj, k_arr, k_cb, v_arr, v_cb, nb, seq, _pick(seq, (512, 256, 128)))
    merged = _merge(o_a, o_b, o_c, proj, wts, tm=_pick(t, (256, 128)))
    x1, h2 = _outproj(merged, wts["w_out"], x2d, wts["norm_ffn_g"], tm=_pick(t, (512, 256, 128)))
    qp = _peerq(h2, wts["peer_wq"], tm=_pick(t, (512, 256, 128)))
    y = _peer(qp, h2, x1, wts, tl=_pick(t, (512, 256, 128)), te=1024)
    return y.reshape(nb, seq, d), s_gla, s_gdn, conv_new


def _lanes(vec, start):
    return jnp.zeros((1, SMALL_W), F32).at[0, start:start + vec.shape[0]].set(vec.astype(F32))


def kernel(x_prompt, x_sample, mem_prompt, cache_mem_k, cache_mem_v, state_gla, state_gdn, state_conv, norm_mix_g, norm_mem_g, w_in, w_gla_gate, b_gla_gate, gla_norm_g, gdn_conv_w, gdn_a_log, gdn_dt_bias, gdn_norm_g, w_mem_kv, w_br_gla, w_br_gdn, w_br_mem, b_gates, w_out, norm_ffn_g, peer_wq, peer_k1, peer_k2, peer_u, peer_v, final_norm_g):
    depth = w_in.shape[0]
    assert depth == 1, "the chain below is written for a single layer"
    l = 0
    off, cols = 0, {}
    for name, n in zip(IN_NAMES, IN_SPLITS):
        cols[name] = w_in[l][:, off:off + n]
        off += n
    w_main = jnp.concatenate([cols[n] for n in MAIN_NAMES], axis=1).astype(BF16)
    pad = jnp.zeros((D_MODEL, SMALL_W - GLA_RANK - 2 * GDN_HEADS), F32)
    w_small = jnp.concatenate([cols["glr"], cols["da"], cols["db"], pad], axis=1).astype(BF16)
    wts = {
        "norm_mix_g": norm_mix_g[l][None], "w_main": w_main, "w_small": w_small,
        "w_gla_gate": w_gla_gate[l], "b_gla_gate": b_gla_gate[l][None], "gla_norm_g": gla_norm_g[l][None],
        "gdn_conv_w": gdn_conv_w[l], "a_log_lanes": _lanes(gdn_a_log[l], DA_LANE),
        "dt_bias_lanes": _lanes(gdn_dt_bias[l], DA_LANE), "gdn_norm_g": gdn_norm_g[l][None],
        "w_br_gla": w_br_gla[l].astype(BF16), "w_br_gdn": w_br_gdn[l].astype(BF16),
        "w_br_mem": w_br_mem[l].astype(BF16), "b_gates": b_gates[l][None], "w_out": w_out[l].astype(BF16),
        "norm_ffn_g": norm_ffn_g[l][None], "peer_wq": peer_wq[l].astype(BF16),
        "peer_k1": peer_k1[l].astype(BF16), "peer_k2": peer_k2[l].astype(BF16),
        "peer_u": peer_u[l].astype(BF16), "peer_vt": peer_v[l].T.astype(BF16),
        "final_norm_g": final_norm_g[None],
    }
    nb_p, n_mem, _ = mem_prompt.shape
    nb_s = x_sample.shape[0]

    kv = _normmm(mem_prompt.reshape(nb_p * n_mem, D_MODEL), norm_mem_g[l][None], w_mem_kv[l].astype(BF16),
                 tm=_pick(nb_p * n_mem, (512, 256)), tn=1024)
    kv3 = kv.reshape(nb_p, n_mem, 2 * MEM_W)
    y_p, gla_p, gdn_p, conv_p = _group(x_prompt, kv3, 0, kv3, 1, None, None, None, wts)
    mk_p = kv3[:, :, :MEM_W].reshape(nb_p, n_mem, MEM_HEADS, MEM_DH)
    mv_p = kv3[:, :, MEM_W:].reshape(nb_p, n_mem, MEM_HEADS, MEM_DH)

    ck = cache_mem_k.reshape(nb_s, n_mem, MEM_HEADS, MEM_DH)
    cv = cache_mem_v.reshape(nb_s, n_mem, MEM_HEADS, MEM_DH)
    y_s, gla_s, gdn_s, conv_s = _group(x_sample, ck, 0, cv, 0, state_gla[l], state_gdn[l], state_conv[l], wts)

    return (y_p, y_s, gla_p[None], gdn_p[None], conv_p[None], mk_p[None], mv_p[None],
            gla_s[None], gdn_s[None], conv_s[None])
```

```python
import functools

import jax
import jax.numpy as jnp
from jax import lax
from jax.experimental import pallas as pl
from jax.experimental.pallas import tpu as pltpu

F32 = jnp.float32
BF16 = jnp.bfloat16
I32 = jnp.int32
HI = lax.Precision.HIGHEST
NEG_INF = float("-inf")

D_MODEL = 2048
EPS = 1e-6
N_MEM = 256
GLA_HEADS, GLA_DK, GLA_DV, GLA_RANK, GLA_TAU = 4, 128, 256, 16, 16.0
GLA_QK, GLA_V = GLA_HEADS * GLA_DK, GLA_HEADS * GLA_DV
GDN_HEADS, GDN_DK, GDN_DV, GDN_CONV = 8, 128, 128, 4
GDN_QK, GDN_V = GDN_HEADS * GDN_DK, GDN_HEADS * GDN_DV
GDN_CONV_CH = 2 * GDN_QK + GDN_V
MEM_HEADS, MEM_DH = 4, 256
MEM_W = MEM_HEADS * MEM_DH
PEER_KEYS, PEER_HEADS, PEER_HALF, PEER_TOPK = 128, 8, 128, 16
PEER_EXPERTS = PEER_KEYS * PEER_KEYS
PEER_SUB = 512
N_GATES = 3 * D_MODEL
IN_SPLITS = (GLA_QK, GLA_QK, GLA_V, GLA_V, GLA_RANK, GDN_CONV_CH, GDN_V, GDN_HEADS, GDN_HEADS, MEM_W, N_GATES)
IN_NAMES = ("gq", "gk", "gv", "gr", "glr", "dqkv", "dz", "da", "db", "mq", "gates")
N_MAIN = 2 * GLA_QK + 2 * GLA_V + GDN_CONV_CH + GDN_V + MEM_W + N_GATES
SMALL_W = 128
DA_LANE, DB_LANE = GLA_RANK, GLA_RANK + GDN_HEADS
_OFF = {name: sum(IN_SPLITS[:i]) for i, name in enumerate(IN_NAMES)}
IN_COLS = sum(IN_SPLITS)
MAIN_RANGES = ((0, _OFF["glr"]), (_OFF["dqkv"], _OFF["da"]), (_OFF["mq"], IN_COLS))
GLR_TILE = _OFF["glr"]
DAB_TILE = _OFF["da"] - DA_LANE
assert GLR_TILE % SMALL_W == 0 and DAB_TILE % SMALL_W == 0 and sum(b - a for a, b in MAIN_RANGES) == N_MAIN

VMEM_LIMIT_BYTES = 56 * 1024 * 1024
SUBLANES, LANES = 8, 128
MXU_DIM = 256


def _cparams(*sem):
    return pltpu.CompilerParams(dimension_semantics=sem, vmem_limit_bytes=VMEM_LIMIT_BYTES)


def _sds(shape, dtype=F32):
    return jax.ShapeDtypeStruct(shape, dtype)


def _mm(a, b, **kw):
    return jnp.dot(a, b, preferred_element_type=F32, **kw)


def _nt(a, b, **kw):
    return lax.dot_general(a, b, (((1,), (1,)), ((), ())), preferred_element_type=F32, **kw)


def _tn(a, b):
    return lax.dot_general(a, b, (((0,), (0,)), ((), ())), preferred_element_type=F32)


def _softplus(x):
    return jnp.maximum(x, 0.0) + jnp.log1p(jnp.exp(-jnp.abs(x)))


def _silu(x):
    return x * jax.nn.sigmoid(x)


def _rms(x, g):
    return x * lax.rsqrt(jnp.mean(x * x, axis=-1, keepdims=True) + EPS) * g


def _regroup_body(w_ref, o_ref, os_ref):
    w = w_ref[...]
    o_ref[...] = jnp.concatenate([w[:, a:b] for a, b in MAIN_RANGES], axis=1).astype(BF16)
    lane = lax.broadcasted_iota(I32, (w.shape[0], SMALL_W), 1)
    glr_tile = w[:, GLR_TILE:GLR_TILE + SMALL_W]
    dab_tile = w[:, DAB_TILE:DAB_TILE + SMALL_W]
    os_ref[...] = jnp.where(lane < DA_LANE, glr_tile, jnp.where(lane < DB_LANE + GDN_HEADS, dab_tile, 0.0)).astype(BF16)


def _regroup(w_in2d, rb):
    k, n = w_in2d.shape
    return pl.pallas_call(
        _regroup_body,
        grid=(k // rb,),
        in_specs=[pl.BlockSpec((rb, n), lambda i: (i, 0))],
        out_specs=[pl.BlockSpec((rb, N_MAIN), lambda i: (i, 0)), pl.BlockSpec((rb, SMALL_W), lambda i: (i, 0))],
        out_shape=[_sds((k, N_MAIN), BF16), _sds((k, SMALL_W), BF16)],
        compiler_params=_cparams("parallel"),
        name="regroup",
    )(w_in2d)


def _transpose_cast_body(v_ref, o_ref):
    o_ref[...] = v_ref[...].T.astype(BF16)


def _transpose_cast(v, tb):
    n, d = v.shape
    return pl.pallas_call(
        _transpose_cast_body,
        grid=(n // tb,),
        in_specs=[pl.BlockSpec((tb, d), lambda i: (i, 0))],
        out_specs=pl.BlockSpec((d, tb), lambda i: (0, i)),
        out_shape=_sds((d, n), BF16),
        compiler_params=_cparams("parallel"),
        name="vtable_t",
    )(v)


def _inproj_body(x_ref, g_ref, w_ref, ws_ref, o_ref, os_ref, xn_ref):
    @pl.when(pl.program_id(1) == 0)
    def _():
        xn = _rms(x_ref[...], g_ref[...]).astype(BF16)
        xn_ref[...] = xn
        os_ref[...] = _mm(xn, ws_ref[...])

    o_ref[...] = _mm(xn_ref[...], w_ref[...])


def _inproj(x2d, g, w_main, w_small, tm, tn):
    t, k = x2d.shape
    n = w_main.shape[1]
    return pl.pallas_call(
        _inproj_body,
        grid=(t // tm, n // tn),
        in_specs=[pl.BlockSpec((tm, k), lambda i, j: (i, 0)),
                  pl.BlockSpec((1, k), lambda i, j: (0, 0)),
                  pl.BlockSpec((k, tn), lambda i, j: (0, j)),
                  pl.BlockSpec((k, SMALL_W), lambda i, j: (0, 0))],
        out_specs=[pl.BlockSpec((tm, tn), lambda i, j: (i, j)),
                   pl.BlockSpec((tm, SMALL_W), lambda i, j: (i, 0))],
        out_shape=[_sds((t, n)), _sds((t, SMALL_W))],
        scratch_shapes=[pltpu.VMEM((tm, k), BF16)],
        compiler_params=_cparams("parallel", "arbitrary"),
        name="inproj",
    )(x2d, g, w_main, w_small)


def _normmm_body(x_ref, g_ref, w_ref, o_ref, xn_ref):
    @pl.when(pl.program_id(1) == 0)
    def _():
        xn_ref[...] = _rms(x_ref[...], g_ref[...]).astype(BF16)

    o_ref[...] = _mm(xn_ref[...], w_ref[...])


def _normmm(x2d, g, w, tm, tn):
    t, k = x2d.shape
    n = w.shape[1]
    return pl.pallas_call(
        _normmm_body,
        grid=(t // tm, n // tn),
        in_specs=[pl.BlockSpec((tm, k), lambda i, j: (i, 0)),
                  pl.BlockSpec((1, k), lambda i, j: (0, 0)),
                  pl.BlockSpec((k, tn), lambda i, j: (0, j))],
        out_specs=pl.BlockSpec((tm, tn), lambda i, j: (i, j)),
        out_shape=_sds((t, n)),
        scratch_shapes=[pltpu.VMEM((tm, k), BF16)],
        compiler_params=_cparams("parallel", "arbitrary"),
        name="memkv",
    )(x2d, g, w)


def _gla_body(*refs, ch, nc, has_state):
    if has_state:
        (q_ref, k_ref, v_ref, r_ref, sm_ref, wg_ref, bg_ref, ng_ref, tri_ref, s0_ref,
         o_ref, so_ref, st_scr, b_scr) = refs
    else:
        (q_ref, k_ref, v_ref, r_ref, sm_ref, wg_ref, bg_ref, ng_ref, tri_ref,
         o_ref, so_ref, st_scr, b_scr) = refs
    step = pl.program_id(1)

    @pl.when(step == 0)
    def _():
        for h in range(GLA_HEADS):
            if has_state:
                st_scr[h] = s0_ref[0, h].T
            else:
                st_scr[h] = jnp.zeros((GLA_DV, GLA_DK), F32)

    logit = _mm(sm_ref[:, 0:GLA_RANK], wg_ref[...], precision=HI) + bg_ref[...]
    g = (jnp.minimum(logit, 0.0) - jnp.log1p(jnp.exp(-jnp.abs(logit)))) * (1.0 / GLA_TAU)
    b_scr[...] = _mm(tri_ref[...], g, precision=HI)

    row_io = lax.broadcasted_iota(I32, (ch, GLA_DK), 0)
    lane_io = lax.broadcasted_iota(I32, (ch, ch), 1)

    def chunk(c, carry):
        rows = pl.ds(pl.multiple_of(c * ch, ch), ch)
        for h in range(GLA_HEADS):
            ksl = slice(h * GLA_DK, (h + 1) * GLA_DK)
            vsl = slice(h * GLA_DV, (h + 1) * GLA_DV)
            qh = q_ref[rows, ksl] * (GLA_DK ** -0.5)
            kh = k_ref[rows, ksl]
            vh = v_ref[rows, vsl]
            bh = b_scr[rows, ksl]
            st = st_scr[h]
            o = _nt(qh * jnp.exp(bh), st)
            att_t = jnp.zeros((ch, ch), F32)
            for i in range(ch):
                dec = jnp.exp(jnp.where(row_io <= i, bh[i:i + 1, :] - bh, NEG_INF))
                col = jnp.sum(kh * dec * qh[i:i + 1, :], axis=1, keepdims=True)
                att_t = jnp.where(lane_io == i, col, att_t)
            o = o + _tn(att_t, vh)
            b_last = bh[ch - 1:ch, :]
            st_scr[h] = jnp.exp(b_last) * st + _tn(vh, kh * jnp.exp(b_last - bh))
            rh = r_ref[rows, vsl]
            o_ref[rows, vsl] = _rms(o, ng_ref[...]) * _silu(rh)
        return carry

    lax.fori_loop(0, nc, chunk, 0)

    @pl.when(step == pl.num_programs(1) - 1)
    def _():
        for h in range(GLA_HEADS):
            so_ref[0, h] = st_scr[h].T


def _gla(proj, small, wts, s0, nb, seq, ch, lt):
    nsteps = seq // lt
    nc = lt // ch
    t = nb * seq
    rix = lambda b, s: b * nsteps + s
    idx = lax.broadcasted_iota(I32, (lt, lt), 0)
    jdx = lax.broadcasted_iota(I32, (lt, lt), 1)
    tri = ((jdx <= idx) & (idx // ch == jdx // ch)).astype(F32)
    in_specs = [pl.BlockSpec((lt, GLA_QK), lambda b, s: (rix(b, s), 0)),
                pl.BlockSpec((lt, GLA_QK), lambda b, s: (rix(b, s), 1)),
                pl.BlockSpec((lt, GLA_V), lambda b, s: (rix(b, s), 1)),
                pl.BlockSpec((lt, GLA_V), lambda b, s: (rix(b, s), 2)),
                pl.BlockSpec((lt, SMALL_W), lambda b, s: (rix(b, s), 0)),
                pl.BlockSpec((GLA_RANK, GLA_QK), lambda b, s: (0, 0)),
                pl.BlockSpec((1, GLA_QK), lambda b, s: (0, 0)),
                pl.BlockSpec((1, GLA_DV), lambda b, s: (0, 0)),
                pl.BlockSpec((lt, lt), lambda b, s: (0, 0))]
    args = [proj, proj, proj, proj, small, wts["w_gla_gate"], wts["b_gla_gate"], wts["gla_norm_g"], tri]
    if s0 is not None:
        in_specs.append(pl.BlockSpec((1, GLA_HEADS, GLA_DK, GLA_DV), lambda b, s: (b, 0, 0, 0)))
        args.append(s0)
    return pl.pallas_call(
        functools.partial(_gla_body, ch=ch, nc=nc, has_state=s0 is not None),
        grid=(nb, nsteps),
        in_specs=in_specs,
        out_specs=[pl.BlockSpec((lt, GLA_V), lambda b, s: (rix(b, s), 0)),
                   pl.BlockSpec((1, GLA_HEADS, GLA_DK, GLA_DV), lambda b, s: (b, 0, 0, 0))],
        out_shape=[_sds((t, GLA_V)), _sds((nb, GLA_HEADS, GLA_DK, GLA_DV))],
        scratch_shapes=[pltpu.VMEM((GLA_HEADS, GLA_DV, GLA_DK), F32), pltpu.VMEM((lt, GLA_QK), F32)],
        compiler_params=_cparams("parallel", "arbitrary"),
        name="gla",
    )(*args)


def _gdn_body(*refs, c, has_state, nsq, grp):
    if has_state:
        (x_ref, z_ref, sm_ref, cw_ref, al_ref, dtb_ref, ng_ref, s0_ref, c0_ref,
         o_ref, so_ref, co_ref, st_scr, xp_scr) = refs
    else:
        (x_ref, z_ref, sm_ref, cw_ref, al_ref, dtb_ref, ng_ref,
         o_ref, so_ref, co_ref, st_scr, xp_scr) = refs
    step = pl.program_id(1)
    hist = GDN_CONV - 1
    base = SUBLANES - hist

    @pl.when(step == 0)
    def _():
        if has_state:
            st_scr[...] = s0_ref[0]
            xp_scr[base:SUBLANES, :] = c0_ref[0]
        else:
            st_scr[...] = jnp.zeros(st_scr.shape, F32)
            xp_scr[base:SUBLANES, :] = jnp.zeros((hist, GDN_CONV_CH), F32)

    xp_scr[SUBLANES:SUBLANES + c, :] = x_ref[...]
    conv = xp_scr[pl.ds(base, c), :] * cw_ref[0:1, :]
    for w in range(1, GDN_CONV):
        conv = conv + xp_scr[pl.ds(base + w, c), :] * cw_ref[w:w + 1, :]
    tail = xp_scr[pl.ds(c + base, hist), :]
    xp_scr[base:SUBLANES, :] = tail
    co_ref[0] = tail
    act = _silu(conv)

    sm = sm_ref[...]
    gdec = -jnp.exp(al_ref[...]) * _softplus(sm + dtb_ref[...])
    beta = jax.nn.sigmoid(sm)
    ci = lax.broadcasted_iota(I32, (c, c), 0)
    cj = lax.broadcasted_iota(I32, (c, c), 1)
    bcol = _mm((ci >= cj).astype(F32), gdec, precision=HI)
    rb = (lambda x: x.astype(BF16)) if c >= 2 * SUBLANES else (lambda x: x.astype(BF16).astype(F32))

    rows = grp * c
    shift = c.bit_length() - 1
    ii = lax.broadcasted_iota(I32, (rows, rows), 0)
    jj = lax.broadcasted_iota(I32, (rows, rows), 1)
    same = (ii >> shift) == (jj >> shift)
    incl = same & (ii >= jj)
    strict = same & (ii > jj)
    eye = (ii == jj).astype(F32)
    first = ((lax.broadcasted_iota(I32, (SUBLANES, LANES), 0) == 0)
             & (lax.broadcasted_iota(I32, (SUBLANES, LANES), 1) == 0)).astype(F32)
    stack = lambda parts: jnp.concatenate(parts, axis=0) if len(parts) > 1 else parts[0]

    for g in range(GDN_HEADS // grp):
        heads = list(range(g * grp, (g + 1) * grp))
        q_parts, k_parts, v_parts = [], [], []
        for h in heads:
            cq = act[:, h * GDN_DK:(h + 1) * GDN_DK]
            ck = act[:, GDN_QK + h * GDN_DK:GDN_QK + (h + 1) * GDN_DK]
            q_parts.append(cq * lax.rsqrt(jnp.sum(cq * cq, axis=-1, keepdims=True) + EPS) * (GDN_DK ** -0.5))
            k_parts.append(ck * lax.rsqrt(jnp.sum(ck * ck, axis=-1, keepdims=True) + EPS))
            v_parts.append(act[:, 2 * GDN_QK + h * GDN_DV:2 * GDN_QK + (h + 1) * GDN_DV])
        k_all, v_all = stack(k_parts), stack(v_parts)
        bc = stack([bcol[:, DA_LANE + h:DA_LANE + h + 1] for h in heads])
        bt = stack([beta[:, DB_LANE + h:DB_LANE + h + 1] for h in heads])
        br = _nt(first, jnp.broadcast_to(bc, (rows, LANES)), precision=HI)[0:1, :]
        gam = jnp.exp(jnp.where(incl, bc - br, NEG_INF))
        kq = rb(stack(k_parts + q_parts))
        kkqk = _nt(kq, kq[:rows])
        kk, qk = kkqk[:rows], kkqk[rows:]
        p = -(bt * jnp.where(strict, gam, 0.0) * kk)
        ainv = eye + p
        pk = p
        for _ in range(nsq):
            pkb = rb(pk)
            pk = _mm(pkb, pkb)
            ainv = ainv + _mm(rb(ainv), rb(pk))
        ks_parts, qs_parts = [], []
        for n, h in enumerate(heads):
            kq_h = stack([kq[n * c:(n + 1) * c], kq[rows + n * c:rows + (n + 1) * c]])
            kqs = _mm(kq_h, rb(st_scr[h]))
            ks_parts.append(kqs[:c])
            qs_parts.append(kqs[c:])
        eb = jnp.exp(bc)
        u = rb(_mm(rb(ainv), rb(bt * (v_all - eb * stack(ks_parts)))))
        o = eb * stack(qs_parts) + _mm(rb(qk * gam), u)
        for n, h in enumerate(heads):
            r0, r1 = n * c, (n + 1) * c
            sl = slice(h * GDN_DK, (h + 1) * GDN_DK)
            b_last = bc[r1 - 1:r1, :]
            st_scr[h] = jnp.exp(b_last) * st_scr[h] + _tn(rb(k_all[r0:r1] * jnp.exp(b_last - bc[r0:r1])), u[r0:r1])
            o_ref[:, sl] = _rms(o[r0:r1], ng_ref[...]) * _silu(z_ref[:, sl])

    @pl.when(step == pl.num_programs(1) - 1)
    def _():
        so_ref[0] = st_scr[...]


def _gdn(proj, small, wts, s0, conv0, nb, seq, c):
    nsteps = seq // c
    t = nb * seq
    rix = lambda b, s: b * nsteps + s
    nsq = max(c.bit_length() - 2, 0)
    grp = min(GDN_HEADS, MXU_DIM // c)
    in_specs = [pl.BlockSpec((c, GDN_CONV_CH), lambda b, s: (rix(b, s), 1)),
                pl.BlockSpec((c, GDN_V), lambda b, s: (rix(b, s), 6)),
                pl.BlockSpec((c, SMALL_W), lambda b, s: (rix(b, s), 0)),
                pl.BlockSpec((GDN_CONV, GDN_CONV_CH), lambda b, s: (0, 0)),
                pl.BlockSpec((1, SMALL_W), lambda b, s: (0, 0)),
                pl.BlockSpec((1, SMALL_W), lambda b, s: (0, 0)),
                pl.BlockSpec((1, GDN_DV), lambda b, s: (0, 0))]
    args = [proj, proj, small, wts["gdn_conv_w"], wts["a_log_lanes"], wts["dt_bias_lanes"], wts["gdn_norm_g"]]
    if s0 is not None:
        in_specs += [pl.BlockSpec((1, GDN_HEADS, GDN_DK, GDN_DV), lambda b, s: (b, 0, 0, 0)),
                     pl.BlockSpec((1, GDN_CONV - 1, GDN_CONV_CH), lambda b, s: (b, 0, 0))]
        args += [s0, conv0]
    return pl.pallas_call(
        functools.partial(_gdn_body, c=c, has_state=s0 is not None, nsq=nsq, grp=grp),
        grid=(nb, nsteps),
        in_specs=in_specs,
        out_specs=[pl.BlockSpec((c, GDN_V), lambda b, s: (rix(b, s), 0)),
                   pl.BlockSpec((1, GDN_HEADS, GDN_DK, GDN_DV), lambda b, s: (b, 0, 0, 0)),
                   pl.BlockSpec((1, GDN_CONV - 1, GDN_CONV_CH), lambda b, s: (b, 0, 0))],
        out_shape=[_sds((t, GDN_V)), _sds((nb, GDN_HEADS, GDN_DK, GDN_DV)), _sds((nb, GDN_CONV - 1, GDN_CONV_CH))],
        scratch_shapes=[pltpu.VMEM((GDN_HEADS, GDN_DK, GDN_DV), F32),
                        pltpu.VMEM((c + SUBLANES, GDN_CONV_CH), F32)],
        compiler_params=_cparams("parallel", "arbitrary"),
        name="gdn",
    )(*args)


def _xattn_body(q_ref, k_ref, v_ref, o_ref, *, per_head_kv):
    for h in range(MEM_HEADS):
        sl = slice(h * MEM_DH, (h + 1) * MEM_DH)
        if per_head_kv:
            kh, vh = k_ref[0, :, h, :], v_ref[0, :, h, :]
        else:
            kh, vh = k_ref[0, :, sl], v_ref[0, :, sl]
        sc = _nt(q_ref[:, sl], kh) * (MEM_DH ** -0.5)
        p = jnp.exp(sc - jnp.max(sc, axis=-1, keepdims=True))
        p = p / jnp.sum(p, axis=-1, keepdims=True)
        o_ref[:, sl] = _mm(p, vh)


def _xattn(proj, k_arr, k_cb, v_arr, v_cb, nb, seq, lt):
    nsteps = seq // lt
    t = nb * seq
    per_head_kv = k_arr.ndim == 4
    if per_head_kv:
        kv_spec = lambda cb: pl.BlockSpec((1, N_MEM, MEM_HEADS, MEM_DH), lambda b, s: (b, 0, 0, 0))
    else:
        kv_spec = lambda cb: pl.BlockSpec((1, N_MEM, MEM_W), lambda b, s: (b, 0, cb))
    return pl.pallas_call(
        functools.partial(_xattn_body, per_head_kv=per_head_kv),
        grid=(nb, nsteps),
        in_specs=[pl.BlockSpec((lt, MEM_W), lambda b, s: (b * nsteps + s, 7)),
                  kv_spec(k_cb), kv_spec(v_cb)],
        out_specs=pl.BlockSpec((lt, MEM_W), lambda b, s: (b * nsteps + s, 0)),
        out_shape=_sds((t, MEM_W)),
        compiler_params=_cparams("parallel", "arbitrary"),
        name="xattn",
    )(proj, k_arr, v_arr)


def _merge_body(oa_ref, ob_ref, oc_ref, ga_ref, gb_ref, gc_ref, ba_ref, bb_ref, bc_ref,
                wa_ref, wb_ref, wc_ref, m_ref):
    acc = jax.nn.sigmoid(ga_ref[...] + ba_ref[...]) * _mm(oa_ref[...].astype(BF16), wa_ref[...])
    acc = acc + jax.nn.sigmoid(gb_ref[...] + bb_ref[...]) * _mm(ob_ref[...].astype(BF16), wb_ref[...])
    acc = acc + jax.nn.sigmoid(gc_ref[...] + bc_ref[...]) * _mm(oc_ref[...].astype(BF16), wc_ref[...])
    m_ref[...] = acc.astype(BF16)


def _merge(o_a, o_b, o_c, proj, wts, tm):
    t = o_a.shape[0]
    row = lambda w: pl.BlockSpec((tm, w), lambda i: (i, 0))
    gate = lambda cb: pl.BlockSpec((tm, D_MODEL), lambda i: (i, cb))
    bias = lambda cb: pl.BlockSpec((1, D_MODEL), lambda i: (0, cb))
    wspec = lambda k: pl.BlockSpec((k, D_MODEL), lambda i: (0, 0))
    return pl.pallas_call(
        _merge_body,
        grid=(t // tm,),
        in_specs=[row(GLA_V), row(GDN_V), row(MEM_W), gate(4), gate(5), gate(6), bias(0), bias(1), bias(2),
                  wspec(GLA_V), wspec(GDN_V), wspec(MEM_W)],
        out_specs=pl.BlockSpec((tm, D_MODEL), lambda i: (i, 0)),
        out_shape=_sds((t, D_MODEL), BF16),
        compiler_params=_cparams("parallel"),
        name="merge",
    )(o_a, o_b, o_c, proj, proj, proj, wts["b_gates"], wts["b_gates"], wts["b_gates"],
      wts["w_br_gla"], wts["w_br_gdn"], wts["w_br_mem"])


def _outproj_body(m_ref, w_ref, x_ref, g_ref, x1_ref, h2_ref):
    x1 = x_ref[...] + _mm(m_ref[...], w_ref[...])
    x1_ref[...] = x1
    h2_ref[...] = _rms(x1, g_ref[...]).astype(BF16)


def _outproj(merged, w_out, x2d, g, tm):
    t = x2d.shape[0]
    blk = lambda: pl.BlockSpec((tm, D_MODEL), lambda i: (i, 0))
    return pl.pallas_call(
        _outproj_body,
        grid=(t // tm,),
        in_specs=[blk(), pl.BlockSpec((D_MODEL, D_MODEL), lambda i: (0, 0)), blk(),
                  pl.BlockSpec((1, D_MODEL), lambda i: (0, 0))],
        out_specs=[blk(), blk()],
        out_shape=[_sds((t, D_MODEL)), _sds((t, D_MODEL), BF16)],
        compiler_params=_cparams("parallel"),
        name="outproj",
    )(merged, w_out, x2d, g)


def _peerq_body(h_ref, w_ref, q_ref):
    q_ref[...] = _mm(h_ref[...], w_ref[...]).astype(BF16)


def _peerq(h2, wq, tm):
    t = h2.shape[0]
    n = wq.shape[1]
    return pl.pallas_call(
        _peerq_body,
        grid=(t // tm,),
        in_specs=[pl.BlockSpec((tm, D_MODEL), lambda i: (i, 0)), pl.BlockSpec((D_MODEL, n), lambda i: (0, 0))],
        out_specs=pl.BlockSpec((tm, n), lambda i: (i, 0)),
        out_shape=_sds((t, n), BF16),
        compiler_params=_cparams("parallel"),
        name="peerq",
    )(h2, wq)


def _top16(x, tv_ref, slot):
    io = lax.broadcasted_iota(I32, x.shape, 0)
    rank = jnp.full(x.shape, PEER_TOPK, I32)
    for r in range(PEER_TOPK):
        m = jnp.max(x, axis=0, keepdims=True)
        idx = jnp.min(jnp.where(x == m, io, x.shape[0]), axis=0, keepdims=True)
        hit = io == idx
        rank = jnp.where(hit, r, rank)
        x = jnp.where(hit, NEG_INF, x)
        tv_ref[slot, r:r + 1, :] = m
    return rank


def _peer_select(h, qp_ref, k1_ref, k2_ref, tv_scr, lh_scr, p1_scr, r2_scr, p2_scr, tl):
    base = pl.multiple_of(h * 2 * PEER_HALF, 2 * PEER_HALF)
    s1 = _nt(k1_ref[...], qp_ref[:, pl.ds(base, PEER_HALF)])
    s2 = _nt(k2_ref[...], qp_ref[:, pl.ds(base + PEER_HALF, PEER_HALF)])
    rank1 = _top16(s1, tv_scr, 0)
    rank2 = _top16(s2, tv_scr, 1)
    t1 = tv_scr[0]
    t2 = tv_scr[1]
    a_io = lax.broadcasted_iota(I32, (PEER_TOPK, tl), 0)
    limit = jnp.zeros((PEER_TOPK, tl), I32)
    for _ in range(PEER_TOPK):
        nxt = jnp.broadcast_to(t2[0:1, :], (PEER_TOPK, tl))
        for b in range(1, PEER_TOPK):
            nxt = jnp.where(limit == b, t2[b:b + 1, :], nxt)
        front = jnp.where(limit < PEER_TOPK, t1 + nxt, NEG_INF)
        m = jnp.max(front, axis=0, keepdims=True)
        a_pick = jnp.min(jnp.where(front == m, a_io, PEER_TOPK), axis=0, keepdims=True)
        limit = limit + (a_io == a_pick).astype(I32)
    p1s = jnp.exp(t1 - t1[0:1, :])
    p2s = jnp.exp(t2 - t2[0:1, :])
    inner = jnp.zeros((PEER_TOPK, tl), F32)
    for b in range(PEER_TOPK):
        inner = inner + jnp.where(limit > b, p2s[b:b + 1, :], 0.0)
    inv_z = 1.0 / jnp.sum(p1s * inner, axis=0, keepdims=True)
    lim_e1 = jnp.zeros(rank1.shape, I32)
    for a in range(PEER_TOPK):
        lim_e1 = jnp.where(rank1 == a, limit[a:a + 1, :], lim_e1)
    lh_scr[h] = lim_e1.astype(F32)
    p1_scr[h] = jnp.where(rank1 < PEER_TOPK, jnp.exp(s1 - t1[0:1, :]), 0.0) * inv_z
    r2_scr[h] = rank2.astype(F32).astype(BF16)
    p2_scr[h] = jnp.exp(jnp.minimum(s2 - t2[0:1, :], 0.0)).astype(BF16)


def _peer_body(qp_ref, h2_ref, x1_ref, k1_ref, k2_ref, u_ref, vt_ref, fg_ref, y_ref,
               lh_scr, p1_scr, r2_scr, p2_scr, tv_scr, acc_scr, sca_scr, scb_scr, *, tl, te):
    e = pl.program_id(1)

    @pl.when(e == 0)
    def _():
        acc_scr[...] = jnp.zeros(acc_scr.shape, F32)

        def one_head(h, carry):
            _peer_select(h, qp_ref, k1_ref, k2_ref, tv_scr, lh_scr, p1_scr, r2_scr, p2_scr, tl)
            return carry

        lax.fori_loop(0, PEER_HEADS, one_head, 0)

    e1_per_sub = PEER_SUB // PEER_KEYS
    n_sub = te // PEER_SUB
    sc_bufs = (sca_scr, scb_scr)

    def pre_act(s):
        sc_bufs[s % 2][...] = _nt(u_ref[s * PEER_SUB:(s + 1) * PEER_SUB, :], h2_ref[...])

    pre_act(0)
    for s in range(n_sub):
        if s + 1 < n_sub:
            pre_act(s + 1)
        sc = sc_bufs[s % 2][...]
        act = (0.5 * sc * (1.0 + lax.erf(sc * 0.7071067811865476))).astype(BF16)
        parts = []
        for i in range(e1_per_sub):
            e1 = e * (te // PEER_KEYS) + s * e1_per_sub + i
            coef = jnp.zeros((PEER_KEYS, tl), BF16)
            for h in range(PEER_HEADS):
                lim = lh_scr[h, pl.ds(e1, 1), :].astype(BF16)
                p1 = p1_scr[h, pl.ds(e1, 1), :].astype(BF16)
                coef = coef + jnp.where(r2_scr[h] < lim, p2_scr[h], jnp.zeros((), BF16)) * p1
            parts.append(coef * act[i * PEER_KEYS:(i + 1) * PEER_KEYS, :])
        z = jnp.concatenate(parts, axis=0)
        acc_scr[...] += _mm(vt_ref[:, s * PEER_SUB:(s + 1) * PEER_SUB], z)

    @pl.when(e == pl.num_programs(1) - 1)
    def _():
        y_ref[...] = _rms(x1_ref[...] + acc_scr[...].T, fg_ref[...])


def _peer(qp, h2, x1, wts, tl, te):
    t = h2.shape[0]
    once = pl.Buffered(1)
    keys = lambda: pl.BlockSpec((PEER_KEYS, PEER_HALF), lambda i, e: (0, 0))
    sel = lambda dt=F32: pltpu.VMEM((PEER_HEADS, PEER_KEYS, tl), dt)
    return pl.pallas_call(
        functools.partial(_peer_body, tl=tl, te=te),
        grid=(t // tl, PEER_EXPERTS // te),
        in_specs=[pl.BlockSpec((tl, D_MODEL), lambda i, e: (i, 0), pipeline_mode=once),
                  pl.BlockSpec((tl, D_MODEL), lambda i, e: (i, 0)),
                  pl.BlockSpec((tl, D_MODEL), lambda i, e: (i, 0), pipeline_mode=once),
                  keys(), keys(),
                  pl.BlockSpec((te, D_MODEL), lambda i, e: (e, 0)),
                  pl.BlockSpec((D_MODEL, te), lambda i, e: (0, e)),
                  pl.BlockSpec((1, D_MODEL), lambda i, e: (0, 0))],
        out_specs=pl.BlockSpec((tl, D_MODEL), lambda i, e: (i, 0)),
        out_shape=_sds((t, D_MODEL)),
        scratch_shapes=[sel(), sel(), sel(BF16), sel(BF16), pltpu.VMEM((2, PEER_TOPK, tl), F32),
                        pltpu.VMEM((D_MODEL, tl), F32), pltpu.VMEM((PEER_SUB, tl), F32),
                        pltpu.VMEM((PEER_SUB, tl), F32)],
        compiler_params=_cparams("parallel", "arbitrary"),
        name="peer",
    )(qp, h2, x1, wts["peer_k1"], wts["peer_k2"], wts["peer_u"], wts["peer_vt"], wts["final_norm_g"])


def _pick(n, prefs):
    for p in prefs:
        if n % p == 0:
            return p
    return n


def _group(x3, k_arr, k_cb, v_arr, v_cb, s_gla0, s_gdn0, conv0, wts):
    nb, seq, d = x3.shape
    t = nb * seq
    x2d = x3.reshape(t, d)
    proj, small = _inproj(x2d, wts["norm_mix_g"], wts["w_main"], wts["w_small"],
                          tm=_pick(t, (1024, 512, 256, 128)), tn=1024)
    gla_ch = 16 if seq % 16 == 0 else SUBLANES
    gla_lt = _pick(seq, (256, 128, 64, 32, 16))
    o_a, s_gla = _gla(proj, small, wts, s_gla0, nb, seq, gla_ch, gla_lt)
    o_b, s_gdn, conv_new = _gdn(proj, small, wts, s_gdn0, conv0, nb, seq, _pick(seq, (64, 32, 16)))
    o_c = _xattn(proj, k_arr, k_cb, v_arr, v_cb, nb, seq, _pick(seq, (512, 256, 128)))
    merged = _merge(o_a, o_b, o_c, proj, wts, tm=_pick(t, (256, 128)))
    x1, h2 = _outproj(merged, wts["w_out"], x2d, wts["norm_ffn_g"], tm=_pick(t, (512, 256, 128)))
    qp = _peerq(h2, wts["peer_wq"], tm=_pick(t, (512, 256, 128)))
    y = _peer(qp, h2, x1, wts, tl=_pick(t, (512, 256, 128)), te=1024)
    return y.reshape(nb, seq, d), s_gla, s_gdn, conv_new


def _lanes(vec, start):
    return jnp.zeros((1, SMALL_W), F32).at[0, start:start + vec.shape[0]].set(vec.astype(F32))


def kernel(x_prompt, x_sample, mem_prompt, cache_mem_k, cache_mem_v, state_gla, state_gdn, state_conv, norm_mix_g, norm_mem_g, w_in, w_gla_gate, b_gla_gate, gla_norm_g, gdn_conv_w, gdn_a_log, gdn_dt_bias, gdn_norm_g, w_mem_kv, w_br_gla, w_br_gdn, w_br_mem, b_gates, w_out, norm_ffn_g, peer_wq, peer_k1, peer_k2, peer_u, peer_v, final_norm_g):
    depth = w_in.shape[0]
    assert depth == 1, "the chain below is written for a single layer"
    l = 0
    w_main, w_small = _regroup(w_in.reshape(D_MODEL, IN_COLS), rb=128)
    wts = {
        "norm_mix_g": norm_mix_g[l][None], "w_main": w_main, "w_small": w_small,
        "w_gla_gate": w_gla_gate[l], "b_gla_gate": b_gla_gate[l][None], "gla_norm_g": gla_norm_g[l][None],
        "gdn_conv_w": gdn_conv_w[l], "a_log_lanes": _lanes(gdn_a_log[l], DA_LANE),
        "dt_bias_lanes": _lanes(gdn_dt_bias[l], DA_LANE), "gdn_norm_g": gdn_norm_g[l][None],
        "w_br_gla": w_br_gla[l].astype(BF16), "w_br_gdn": w_br_gdn[l].astype(BF16),
        "w_br_mem": w_br_mem[l].astype(BF16), "b_gates": b_gates[l][None], "w_out": w_out[l].astype(BF16),
        "norm_ffn_g": norm_ffn_g[l][None], "peer_wq": peer_wq[l].astype(BF16),
        "peer_k1": peer_k1[l].astype(BF16), "peer_k2": peer_k2[l].astype(BF16),
        "peer_u": peer_u[l].astype(BF16), "peer_vt": _transpose_cast(peer_v.reshape(PEER_EXPERTS, D_MODEL), tb=512),
        "final_norm_g": final_norm_g[None],
    }
    nb_p, n_mem, _ = mem_prompt.shape
    nb_s = x_sample.shape[0]

    kv = _normmm(mem_prompt.reshape(nb_p * n_mem, D_MODEL), norm_mem_g[l][None], w_mem_kv[l].astype(BF16),
                 tm=_pick(nb_p * n_mem, (512, 256)), tn=1024)
    kv3 = kv.reshape(nb_p, n_mem, 2 * MEM_W)
    y_p, gla_p, gdn_p, conv_p = _group(x_prompt, kv3, 0, kv3, 1, None, None, None, wts)
    mk_p = kv3[:, :, :MEM_W].reshape(nb_p, n_mem, MEM_HEADS, MEM_DH)
    mv_p = kv3[:, :, MEM_W:].reshape(nb_p, n_mem, MEM_HEADS, MEM_DH)

    ck = cache_mem_k.reshape(nb_s, n_mem, MEM_HEADS, MEM_DH)
    cv = cache_mem_v.reshape(nb_s, n_mem, MEM_HEADS, MEM_DH)
    y_s, gla_s, gdn_s, conv_s = _group(x_sample, ck, 0, cv, 0, state_gla[l], state_gdn[l], state_conv[l], wts)

    return (y_p, y_s, gla_p[None], gdn_p[None], conv_p[None], mk_p[None], mv_p[None],
            gla_s[None], gdn_s[None], conv_s[None])
```

```python
import functools

import jax
import jax.numpy as jnp
from jax import lax
from jax.experimental import pallas as pl
from jax.experimental.pallas import tpu as pltpu

F32 = jnp.float32
BF16 = jnp.bfloat16
I32 = jnp.int32
HI = lax.Precision.HIGHEST
NEG_INF = float("-inf")

D_MODEL = 2048
EPS = 1e-6
N_MEM = 256
GLA_HEADS, GLA_DK, GLA_DV, GLA_RANK, GLA_TAU = 4, 128, 256, 16, 16.0
GLA_QK, GLA_V = GLA_HEADS * GLA_DK, GLA_HEADS * GLA_DV
GDN_HEADS, GDN_DK, GDN_DV, GDN_CONV = 8, 128, 128, 4
GDN_QK, GDN_V = GDN_HEADS * GDN_DK, GDN_HEADS * GDN_DV
GDN_CONV_CH = 2 * GDN_QK + GDN_V
MEM_HEADS, MEM_DH = 4, 256
MEM_W = MEM_HEADS * MEM_DH
PEER_KEYS, PEER_HEADS, PEER_HALF, PEER_TOPK = 128, 8, 128, 16
PEER_EXPERTS = PEER_KEYS * PEER_KEYS
PEER_SUB = 512
N_GATES = 3 * D_MODEL
IN_SPLITS = (GLA_QK, GLA_QK, GLA_V, GLA_V, GLA_RANK, GDN_CONV_CH, GDN_V, GDN_HEADS, GDN_HEADS, MEM_W, N_GATES)
IN_NAMES = ("gq", "gk", "gv", "gr", "glr", "dqkv", "dz", "da", "db", "mq", "gates")
N_MAIN = 2 * GLA_QK + 2 * GLA_V + GDN_CONV_CH + GDN_V + MEM_W + N_GATES
SMALL_W = 128
DA_LANE, DB_LANE = GLA_RANK, GLA_RANK + GDN_HEADS
_OFF = {name: sum(IN_SPLITS[:i]) for i, name in enumerate(IN_NAMES)}
IN_COLS = sum(IN_SPLITS)
MAIN_RANGES = ((0, _OFF["glr"]), (_OFF["dqkv"], _OFF["da"]), (_OFF["mq"], IN_COLS))
GLR_TILE = _OFF["glr"]
DAB_TILE = _OFF["da"] - DA_LANE
assert GLR_TILE % SMALL_W == 0 and DAB_TILE % SMALL_W == 0 and sum(b - a for a, b in MAIN_RANGES) == N_MAIN

VMEM_LIMIT_BYTES = 56 * 1024 * 1024
SUBLANES, LANES = 8, 128
MXU_DIM = 256


def _cparams(*sem):
    return pltpu.CompilerParams(dimension_semantics=sem, vmem_limit_bytes=VMEM_LIMIT_BYTES)


def _sds(shape, dtype=F32):
    return jax.ShapeDtypeStruct(shape, dtype)


def _mm(a, b, **kw):
    return jnp.dot(a, b, preferred_element_type=F32, **kw)


def _nt(a, b, **kw):
    return lax.dot_general(a, b, (((1,), (1,)), ((), ())), preferred_element_type=F32, **kw)


def _tn(a, b):
    return lax.dot_general(a, b, (((0,), (0,)), ((), ())), preferred_element_type=F32)


def _softplus(x):
    return jnp.maximum(x, 0.0) + jnp.log1p(jnp.exp(-jnp.abs(x)))


def _silu(x):
    return x * jax.nn.sigmoid(x)


def _rms(x, g):
    return x * lax.rsqrt(jnp.mean(x * x, axis=-1, keepdims=True) + EPS) * g


def _regroup_body(w_ref, o_ref, os_ref):
    w = w_ref[...]
    o_ref[...] = jnp.concatenate([w[:, a:b] for a, b in MAIN_RANGES], axis=1).astype(BF16)
    lane = lax.broadcasted_iota(I32, (w.shape[0], SMALL_W), 1)
    glr_tile = w[:, GLR_TILE:GLR_TILE + SMALL_W]
    dab_tile = w[:, DAB_TILE:DAB_TILE + SMALL_W]
    os_ref[...] = jnp.where(lane < DA_LANE, glr_tile, jnp.where(lane < DB_LANE + GDN_HEADS, dab_tile, 0.0)).astype(BF16)


def _regroup(w_in2d, rb):
    k, n = w_in2d.shape
    return pl.pallas_call(
        _regroup_body,
        grid=(k // rb,),
        in_specs=[pl.BlockSpec((rb, n), lambda i: (i, 0))],
        out_specs=[pl.BlockSpec((rb, N_MAIN), lambda i: (i, 0)), pl.BlockSpec((rb, SMALL_W), lambda i: (i, 0))],
        out_shape=[_sds((k, N_MAIN), BF16), _sds((k, SMALL_W), BF16)],
        compiler_params=_cparams("parallel"),
        name="regroup",
    )(w_in2d)


def _transpose_cast_body(v_ref, o_ref):
    o_ref[...] = v_ref[...].T.astype(BF16)


def _transpose_cast(v, tb):
    n, d = v.shape
    return pl.pallas_call(
        _transpose_cast_body,
        grid=(n // tb,),
        in_specs=[pl.BlockSpec((tb, d), lambda i: (i, 0))],
        out_specs=pl.BlockSpec((d, tb), lambda i: (0, i)),
        out_shape=_sds((d, n), BF16),
        compiler_params=_cparams("parallel"),
        name="vtable_t",
    )(v)


def _inproj_body(x_ref, g_ref, w_ref, ws_ref, o_ref, os_ref, xn_ref):
    @pl.when(pl.program_id(1) == 0)
    def _():
        xn = _rms(x_ref[...], g_ref[...]).astype(BF16)
        xn_ref[...] = xn
        os_ref[...] = _mm(xn, ws_ref[...])

    o_ref[...] = _mm(xn_ref[...], w_ref[...])


def _inproj(x2d, g, w_main, w_small, tm, tn):
    t, k = x2d.shape
    n = w_main.shape[1]
    return pl.pallas_call(
        _inproj_body,
        grid=(t // tm, n // tn),
        in_specs=[pl.BlockSpec((tm, k), lambda i, j: (i, 0)),
                  pl.BlockSpec((1, k), lambda i, j: (0, 0)),
                  pl.BlockSpec((k, tn), lambda i, j: (0, j)),
                  pl.BlockSpec((k, SMALL_W), lambda i, j: (0, 0))],
        out_specs=[pl.BlockSpec((tm, tn), lambda i, j: (i, j)),
                   pl.BlockSpec((tm, SMALL_W), lambda i, j: (i, 0))],
        out_shape=[_sds((t, n)), _sds((t, SMALL_W))],
        scratch_shapes=[pltpu.VMEM((tm, k), BF16)],
        compiler_params=_cparams("parallel", "arbitrary"),
        name="inproj",
    )(x2d, g, w_main, w_small)


def _normmm_body(x_ref, g_ref, w_ref, o_ref, xn_ref):
    @pl.when(pl.program_id(1) == 0)
    def _():
        xn_ref[...] = _rms(x_ref[...], g_ref[...]).astype(BF16)

    o_ref[...] = _mm(xn_ref[...], w_ref[...])


def _normmm(x2d, g, w, tm, tn):
    t, k = x2d.shape
    n = w.shape[1]
    return pl.pallas_call(
        _normmm_body,
        grid=(t // tm, n // tn),
        in_specs=[pl.BlockSpec((tm, k), lambda i, j: (i, 0)),
                  pl.BlockSpec((1, k), lambda i, j: (0, 0)),
                  pl.BlockSpec((k, tn), lambda i, j: (0, j))],
        out_specs=pl.BlockSpec((tm, tn), lambda i, j: (i, j)),
        out_shape=_sds((t, n)),
        scratch_shapes=[pltpu.VMEM((tm, k), BF16)],
        compiler_params=_cparams("parallel", "arbitrary"),
        name="memkv",
    )(x2d, g, w)


def _gla_body(*refs, ch, nc, has_state):
    if has_state:
        (q_ref, k_ref, v_ref, r_ref, sm_ref, wg_ref, bg_ref, ng_ref, tri_ref, s0_ref,
         o_ref, so_ref, st_scr, b_scr) = refs
    else:
        (q_ref, k_ref, v_ref, r_ref, sm_ref, wg_ref, bg_ref, ng_ref, tri_ref,
         o_ref, so_ref, st_scr, b_scr) = refs
    step = pl.program_id(1)

    @pl.when(step == 0)
    def _():
        for h in range(GLA_HEADS):
            if has_state:
                st_scr[h] = s0_ref[0, h].T
            else:
                st_scr[h] = jnp.zeros((GLA_DV, GLA_DK), F32)

    logit = _mm(sm_ref[:, 0:GLA_RANK], wg_ref[...], precision=HI) + bg_ref[...]
    g = (jnp.minimum(logit, 0.0) - jnp.log1p(jnp.exp(-jnp.abs(logit)))) * (1.0 / GLA_TAU)
    b_scr[...] = _mm(tri_ref[...], g, precision=HI)

    rb = (lambda x: x.astype(BF16)) if ch >= 2 * SUBLANES else (lambda x: x)
    row_io = lax.broadcasted_iota(I32, (ch, GLA_DK), 0)
    row_io_v = lax.broadcasted_iota(I32, (ch, GLA_DV), 0)

    def chunk(c, carry):
        rows = pl.ds(pl.multiple_of(c * ch, ch), ch)
        for h in range(GLA_HEADS):
            ksl = slice(h * GLA_DK, (h + 1) * GLA_DK)
            vsl = slice(h * GLA_DV, (h + 1) * GLA_DV)
            qh = q_ref[rows, ksl] * (GLA_DK ** -0.5)
            kh = k_ref[rows, ksl]
            vh = v_ref[rows, vsl]
            bh = b_scr[rows, ksl]
            st = st_scr[h]
            o = _nt(rb(qh * jnp.exp(bh)), rb(st))
            for i in range(ch):
                dec = jnp.exp(jnp.where(row_io <= i, bh[i:i + 1, :] - bh, NEG_INF))
                col = jnp.sum(kh * dec * qh[i:i + 1, :], axis=1, keepdims=True)
                o = jnp.where(row_io_v == i, o + jnp.sum(col * vh, axis=0, keepdims=True), o)
            b_last = bh[ch - 1:ch, :]
            st_scr[h] = jnp.exp(b_last) * st + _tn(rb(vh), rb(kh * jnp.exp(b_last - bh)))
            rh = r_ref[rows, vsl]
            o_ref[rows, vsl] = _rms(o, ng_ref[...]) * _silu(rh)
        return carry

    lax.fori_loop(0, nc, chunk, 0)

    @pl.when(step == pl.num_programs(1) - 1)
    def _():
        for h in range(GLA_HEADS):
            so_ref[0, h] = st_scr[h].T


def _gla(proj, small, wts, s0, nb, seq, ch, lt):
    nsteps = seq // lt
    nc = lt // ch
    t = nb * seq
    rix = lambda b, s: b * nsteps + s
    idx = lax.broadcasted_iota(I32, (lt, lt), 0)
    jdx = lax.broadcasted_iota(I32, (lt, lt), 1)
    tri = ((jdx <= idx) & (idx // ch == jdx // ch)).astype(F32)
    in_specs = [pl.BlockSpec((lt, GLA_QK), lambda b, s: (rix(b, s), 0)),
                pl.BlockSpec((lt, GLA_QK), lambda b, s: (rix(b, s), 1)),
                pl.BlockSpec((lt, GLA_V), lambda b, s: (rix(b, s), 1)),
                pl.BlockSpec((lt, GLA_V), lambda b, s: (rix(b, s), 2)),
                pl.BlockSpec((lt, SMALL_W), lambda b, s: (rix(b, s), 0)),
                pl.BlockSpec((GLA_RANK, GLA_QK), lambda b, s: (0, 0)),
                pl.BlockSpec((1, GLA_QK), lambda b, s: (0, 0)),
                pl.BlockSpec((1, GLA_DV), lambda b, s: (0, 0)),
                pl.BlockSpec((lt, lt), lambda b, s: (0, 0))]
    args = [proj, proj, proj, proj, small, wts["w_gla_gate"], wts["b_gla_gate"], wts["gla_norm_g"], tri]
    if s0 is not None:
        in_specs.append(pl.BlockSpec((1, GLA_HEADS, GLA_DK, GLA_DV), lambda b, s: (b, 0, 0, 0)))
        args.append(s0)
    return pl.pallas_call(
        functools.partial(_gla_body, ch=ch, nc=nc, has_state=s0 is not None),
        grid=(nb, nsteps),
        in_specs=in_specs,
        out_specs=[pl.BlockSpec((lt, GLA_V), lambda b, s: (rix(b, s), 0)),
                   pl.BlockSpec((1, GLA_HEADS, GLA_DK, GLA_DV), lambda b, s: (b, 0, 0, 0))],
        out_shape=[_sds((t, GLA_V)), _sds((nb, GLA_HEADS, GLA_DK, GLA_DV))],
        scratch_shapes=[pltpu.VMEM((GLA_HEADS, GLA_DV, GLA_DK), F32), pltpu.VMEM((lt, GLA_QK), F32)],
        compiler_params=_cparams("parallel", "arbitrary"),
        name="gla",
    )(*args)


def _gdn_body(*refs, c, has_state, nsq, grp):
    if has_state:
        (x_ref, z_ref, sm_ref, cw_ref, al_ref, dtb_ref, ng_ref, s0_ref, c0_ref,
         o_ref, so_ref, co_ref, st_scr, xp_scr) = refs
    else:
        (x_ref, z_ref, sm_ref, cw_ref, al_ref, dtb_ref, ng_ref,
         o_ref, so_ref, co_ref, st_scr, xp_scr) = refs
    step = pl.program_id(1)
    hist = GDN_CONV - 1
    base = SUBLANES - hist

    @pl.when(step == 0)
    def _():
        if has_state:
            st_scr[...] = s0_ref[0]
            xp_scr[base:SUBLANES, :] = c0_ref[0]
        else:
            st_scr[...] = jnp.zeros(st_scr.shape, F32)
            xp_scr[base:SUBLANES, :] = jnp.zeros((hist, GDN_CONV_CH), F32)

    xp_scr[SUBLANES:SUBLANES + c, :] = x_ref[...]
    conv = xp_scr[pl.ds(base, c), :] * cw_ref[0:1, :]
    for w in range(1, GDN_CONV):
        conv = conv + xp_scr[pl.ds(base + w, c), :] * cw_ref[w:w + 1, :]
    tail = xp_scr[pl.ds(c + base, hist), :]
    xp_scr[base:SUBLANES, :] = tail
    co_ref[0] = tail
    act = _silu(conv)

    sm = sm_ref[...]
    gdec = -jnp.exp(al_ref[...]) * _softplus(sm + dtb_ref[...])
    beta = jax.nn.sigmoid(sm)
    ci = lax.broadcasted_iota(I32, (c, c), 0)
    cj = lax.broadcasted_iota(I32, (c, c), 1)
    bcol = _mm((ci >= cj).astype(F32), gdec, precision=HI)
    rb = (lambda x: x.astype(BF16)) if c >= 2 * SUBLANES else (lambda x: x.astype(BF16).astype(F32))

    rows = grp * c
    shift = c.bit_length() - 1
    ii = lax.broadcasted_iota(I32, (rows, rows), 0)
    jj = lax.broadcasted_iota(I32, (rows, rows), 1)
    same = (ii >> shift) == (jj >> shift)
    incl = same & (ii >= jj)
    strict = same & (ii > jj)
    eye = (ii == jj).astype(F32)
    first = ((lax.broadcasted_iota(I32, (SUBLANES, LANES), 0) == 0)
             & (lax.broadcasted_iota(I32, (SUBLANES, LANES), 1) == 0)).astype(F32)
    stack = lambda parts: jnp.concatenate(parts, axis=0) if len(parts) > 1 else parts[0]

    for g in range(GDN_HEADS // grp):
        heads = list(range(g * grp, (g + 1) * grp))
        q_parts, k_parts, v_parts = [], [], []
        for h in heads:
            cq = act[:, h * GDN_DK:(h + 1) * GDN_DK]
            ck = act[:, GDN_QK + h * GDN_DK:GDN_QK + (h + 1) * GDN_DK]
            q_parts.append(cq * lax.rsqrt(jnp.sum(cq * cq, axis=-1, keepdims=True) + EPS) * (GDN_DK ** -0.5))
            k_parts.append(ck * lax.rsqrt(jnp.sum(ck * ck, axis=-1, keepdims=True) + EPS))
            v_parts.append(act[:, 2 * GDN_QK + h * GDN_DV:2 * GDN_QK + (h + 1) * GDN_DV])
        k_all, v_all = stack(k_parts), stack(v_parts)
        bc = stack([bcol[:, DA_LANE + h:DA_LANE + h + 1] for h in heads])
        bt = stack([beta[:, DB_LANE + h:DB_LANE + h + 1] for h in heads])
        br = _nt(first, jnp.broadcast_to(bc, (rows, LANES)), precision=HI)[0:1, :]
        gam = jnp.exp(jnp.where(incl, bc - br, NEG_INF))
        kq = rb(stack(k_parts + q_parts))
        kkqk = _nt(kq, kq[:rows])
        kk, qk = kkqk[:rows], kkqk[rows:]
        p = -(bt * jnp.where(strict, gam, 0.0) * kk)
        ainv = eye + p
        pk = p
        for _ in range(nsq):
            pkb = rb(pk)
            pk = _mm(pkb, pkb)
            ainv = ainv + _mm(rb(ainv), rb(pk))
        ks_parts, qs_parts = [], []
        for n, h in enumerate(heads):
            kq_h = stack([kq[n * c:(n + 1) * c], kq[rows + n * c:rows + (n + 1) * c]])
            kqs = _mm(kq_h, rb(st_scr[h]))
            ks_parts.append(kqs[:c])
            qs_parts.append(kqs[c:])
        eb = jnp.exp(bc)
        u = rb(_mm(rb(ainv), rb(bt * (v_all - eb * stack(ks_parts)))))
        o = eb * stack(qs_parts) + _mm(rb(qk * gam), u)
        for n, h in enumerate(heads):
            r0, r1 = n * c, (n + 1) * c
            sl = slice(h * GDN_DK, (h + 1) * GDN_DK)
            b_last = bc[r1 - 1:r1, :]
            st_scr[h] = jnp.exp(b_last) * st_scr[h] + _tn(rb(k_all[r0:r1] * jnp.exp(b_last - bc[r0:r1])), u[r0:r1])
            o_ref[:, sl] = _rms(o[r0:r1], ng_ref[...]) * _silu(z_ref[:, sl])

    @pl.when(step == pl.num_programs(1) - 1)
    def _():
        so_ref[0] = st_scr[...]


def _gdn(proj, small, wts, s0, conv0, nb, seq, c):
    nsteps = seq // c
    t = nb * seq
    rix = lambda b, s: b * nsteps + s
    nsq = max(c.bit_length() - 2, 0)
    grp = min(GDN_HEADS, MXU_DIM // c)
    in_specs = [pl.BlockSpec((c, GDN_CONV_CH), lambda b, s: (rix(b, s), 1)),
                pl.BlockSpec((c, GDN_V), lambda b, s: (rix(b, s), 6)),
                pl.BlockSpec((c, SMALL_W), lambda b, s: (rix(b, s), 0)),
                pl.BlockSpec((GDN_CONV, GDN_CONV_CH), lambda b, s: (0, 0)),
                pl.BlockSpec((1, SMALL_W), lambda b, s: (0, 0)),
                pl.BlockSpec((1, SMALL_W), lambda b, s: (0, 0)),
                pl.BlockSpec((1, GDN_DV), lambda b, s: (0, 0))]
    args = [proj, proj, small, wts["gdn_conv_w"], wts["a_log_lanes"], wts["dt_bias_lanes"], wts["gdn_norm_g"]]
    if s0 is not None:
        in_specs += [pl.BlockSpec((1, GDN_HEADS, GDN_DK, GDN_DV), lambda b, s: (b, 0, 0, 0)),
                     pl.BlockSpec((1, GDN_CONV - 1, GDN_CONV_CH), lambda b, s: (b, 0, 0))]
        args += [s0, conv0]
    return pl.pallas_call(
        functools.partial(_gdn_body, c=c, has_state=s0 is not None, nsq=nsq, grp=grp),
        grid=(nb, nsteps),
        in_specs=in_specs,
        out_specs=[pl.BlockSpec((c, GDN_V), lambda b, s: (rix(b, s), 0)),
                   pl.BlockSpec((1, GDN_HEADS, GDN_DK, GDN_DV), lambda b, s: (b, 0, 0, 0)),
                   pl.BlockSpec((1, GDN_CONV - 1, GDN_CONV_CH), lambda b, s: (b, 0, 0))],
        out_shape=[_sds((t, GDN_V)), _sds((nb, GDN_HEADS, GDN_DK, GDN_DV)), _sds((nb, GDN_CONV - 1, GDN_CONV_CH))],
        scratch_shapes=[pltpu.VMEM((GDN_HEADS, GDN_DK, GDN_DV), F32),
                        pltpu.VMEM((c + SUBLANES, GDN_CONV_CH), F32)],
        compiler_params=_cparams("parallel", "arbitrary"),
        name="gdn",
    )(*args)


def _xattn_body(q_ref, k_ref, v_ref, o_ref, *, per_head_kv):
    for h in range(MEM_HEADS):
        sl = slice(h * MEM_DH, (h + 1) * MEM_DH)
        if per_head_kv:
            kh, vh = k_ref[0, :, h, :], v_ref[0, :, h, :]
        else:
            kh, vh = k_ref[0, :, sl], v_ref[0, :, sl]
        sc = _nt(q_ref[:, sl], kh) * (MEM_DH ** -0.5)
        p = jnp.exp(sc - jnp.max(sc, axis=-1, keepdims=True))
        p = p / jnp.sum(p, axis=-1, keepdims=True)
        o_ref[:, sl] = _mm(p, vh)


def _xattn(proj, k_arr, k_cb, v_arr, v_cb, nb, seq, lt):
    nsteps = seq // lt
    t = nb * seq
    per_head_kv = k_arr.ndim == 4
    if per_head_kv:
        kv_spec = lambda cb: pl.BlockSpec((1, N_MEM, MEM_HEADS, MEM_DH), lambda b, s: (b, 0, 0, 0))
    else:
        kv_spec = lambda cb: pl.BlockSpec((1, N_MEM, MEM_W), lambda b, s: (b, 0, cb))
    return pl.pallas_call(
        functools.partial(_xattn_body, per_head_kv=per_head_kv),
        grid=(nb, nsteps),
        in_specs=[pl.BlockSpec((lt, MEM_W), lambda b, s: (b * nsteps + s, 7)),
                  kv_spec(k_cb), kv_spec(v_cb)],
        out_specs=pl.BlockSpec((lt, MEM_W), lambda b, s: (b * nsteps + s, 0)),
        out_shape=_sds((t, MEM_W)),
        compiler_params=_cparams("parallel", "arbitrary"),
        name="xattn",
    )(proj, k_arr, v_arr)


def _merge_body(oa_ref, ob_ref, oc_ref, ga_ref, gb_ref, gc_ref, ba_ref, bb_ref, bc_ref,
                wa_ref, wb_ref, wc_ref, m_ref):
    acc = jax.nn.sigmoid(ga_ref[...] + ba_ref[...]) * _mm(oa_ref[...].astype(BF16), wa_ref[...])
    acc = acc + jax.nn.sigmoid(gb_ref[...] + bb_ref[...]) * _mm(ob_ref[...].astype(BF16), wb_ref[...])
    acc = acc + jax.nn.sigmoid(gc_ref[...] + bc_ref[...]) * _mm(oc_ref[...].astype(BF16), wc_ref[...])
    m_ref[...] = acc.astype(BF16)


def _merge(o_a, o_b, o_c, proj, wts, tm):
    t = o_a.shape[0]
    row = lambda w: pl.BlockSpec((tm, w), lambda i: (i, 0))
    gate = lambda cb: pl.BlockSpec((tm, D_MODEL), lambda i: (i, cb))
    bias = lambda cb: pl.BlockSpec((1, D_MODEL), lambda i: (0, cb))
    wspec = lambda k: pl.BlockSpec((k, D_MODEL), lambda i: (0, 0))
    return pl.pallas_call(
        _merge_body,
        grid=(t // tm,),
        in_specs=[row(GLA_V), row(GDN_V), row(MEM_W), gate(4), gate(5), gate(6), bias(0), bias(1), bias(2),
                  wspec(GLA_V), wspec(GDN_V), wspec(MEM_W)],
        out_specs=pl.BlockSpec((tm, D_MODEL), lambda i: (i, 0)),
        out_shape=_sds((t, D_MODEL), BF16),
        compiler_params=_cparams("parallel"),
        name="merge",
    )(o_a, o_b, o_c, proj, proj, proj, wts["b_gates"], wts["b_gates"], wts["b_gates"],
      wts["w_br_gla"], wts["w_br_gdn"], wts["w_br_mem"])


def _outproj_body(m_ref, w_ref, x_ref, g_ref, x1_ref, h2_ref):
    x1 = x_ref[...] + _mm(m_ref[...], w_ref[...])
    x1_ref[...] = x1
    h2_ref[...] = _rms(x1, g_ref[...]).astype(BF16)


def _outproj(merged, w_out, x2d, g, tm):
    t = x2d.shape[0]
    blk = lambda: pl.BlockSpec((tm, D_MODEL), lambda i: (i, 0))
    return pl.pallas_call(
        _outproj_body,
        grid=(t // tm,),
        in_specs=[blk(), pl.BlockSpec((D_MODEL, D_MODEL), lambda i: (0, 0)), blk(),
                  pl.BlockSpec((1, D_MODEL), lambda i: (0, 0))],
        out_specs=[blk(), blk()],
        out_shape=[_sds((t, D_MODEL)), _sds((t, D_MODEL), BF16)],
        compiler_params=_cparams("parallel"),
        name="outproj",
    )(merged, w_out, x2d, g)


def _peerq_body(h_ref, w_ref, q_ref):
    q_ref[...] = _mm(h_ref[...], w_ref[...]).astype(BF16)


def _peerq(h2, wq, tm):
    t = h2.shape[0]
    n = wq.shape[1]
    return pl.pallas_call(
        _peerq_body,
        grid=(t // tm,),
        in_specs=[pl.BlockSpec((tm, D_MODEL), lambda i: (i, 0)), pl.BlockSpec((D_MODEL, n), lambda i: (0, 0))],
        out_specs=pl.BlockSpec((tm, n), lambda i: (i, 0)),
        out_shape=_sds((t, n), BF16),
        compiler_params=_cparams("parallel"),
        name="peerq",
    )(h2, wq)


def _top16(x, tv_ref, slot):
    io = lax.broadcasted_iota(I32, x.shape, 0)
    rank = jnp.full(x.shape, PEER_TOPK, I32)
    for r in range(PEER_TOPK):
        m = jnp.max(x, axis=0, keepdims=True)
        idx = jnp.min(jnp.where(x == m, io, x.shape[0]), axis=0, keepdims=True)
        hit = io == idx
        rank = jnp.where(hit, r, rank)
        x = jnp.where(hit, NEG_INF, x)
        tv_ref[slot, r:r + 1, :] = m
    return rank


def _peer_select(h, qp_ref, k1_ref, k2_ref, tv_scr, lh_scr, p1_scr, r2_scr, p2_scr, tl):
    base = pl.multiple_of(h * 2 * PEER_HALF, 2 * PEER_HALF)
    s1 = _nt(k1_ref[...], qp_ref[:, pl.ds(base, PEER_HALF)])
    s2 = _nt(k2_ref[...], qp_ref[:, pl.ds(base + PEER_HALF, PEER_HALF)])
    rank1 = _top16(s1, tv_scr, 0)
    rank2 = _top16(s2, tv_scr, 1)
    t1 = tv_scr[0]
    t2 = tv_scr[1]
    a_io = lax.broadcasted_iota(I32, (PEER_TOPK, tl), 0)
    limit = jnp.zeros((PEER_TOPK, tl), I32)
    for _ in range(PEER_TOPK):
        nxt = jnp.broadcast_to(t2[0:1, :], (PEER_TOPK, tl))
        for b in range(1, PEER_TOPK):
            nxt = jnp.where(limit == b, t2[b:b + 1, :], nxt)
        front = jnp.where(limit < PEER_TOPK, t1 + nxt, NEG_INF)
        m = jnp.max(front, axis=0, keepdims=True)
        a_pick = jnp.min(jnp.where(front == m, a_io, PEER_TOPK), axis=0, keepdims=True)
        limit = limit + (a_io == a_pick).astype(I32)
    p1s = jnp.exp(t1 - t1[0:1, :])
    p2s = jnp.exp(t2 - t2[0:1, :])
    inner = jnp.zeros((PEER_TOPK, tl), F32)
    for b in range(PEER_TOPK):
        inner = inner + jnp.where(limit > b, p2s[b:b + 1, :], 0.0)
    inv_z = 1.0 / jnp.sum(p1s * inner, axis=0, keepdims=True)
    lim_e1 = jnp.zeros(rank1.shape, I32)
    for a in range(PEER_TOPK):
        lim_e1 = jnp.where(rank1 == a, limit[a:a + 1, :], lim_e1)
    lh_scr[h] = lim_e1.astype(F32)
    p1_scr[h] = jnp.where(rank1 < PEER_TOPK, jnp.exp(s1 - t1[0:1, :]), 0.0) * inv_z
    r2_scr[h] = rank2.astype(F32).astype(BF16)
    p2_scr[h] = jnp.exp(jnp.minimum(s2 - t2[0:1, :], 0.0)).astype(BF16)


def _peer_body(qp_ref, h2_ref, x1_ref, k1_ref, k2_ref, u_ref, vt_ref, fg_ref, y_ref,
               lh_scr, p1_scr, r2_scr, p2_scr, tv_scr, acc_scr, sca_scr, scb_scr, *, tl, te):
    e = pl.program_id(1)

    @pl.when(e == 0)
    def _():
        acc_scr[...] = jnp.zeros(acc_scr.shape, F32)

        def one_head(h, carry):
            _peer_select(h, qp_ref, k1_ref, k2_ref, tv_scr, lh_scr, p1_scr, r2_scr, p2_scr, tl)
            return carry

        lax.fori_loop(0, PEER_HEADS, one_head, 0)

    e1_per_sub = PEER_SUB // PEER_KEYS
    n_sub = te // PEER_SUB
    sc_bufs = (sca_scr, scb_scr)

    def pre_act(s):
        sc_bufs[s % 2][...] = _nt(u_ref[s * PEER_SUB:(s + 1) * PEER_SUB, :], h2_ref[...])

    pre_act(0)
    for s in range(n_sub):
        if s + 1 < n_sub:
            pre_act(s + 1)
        sc = sc_bufs[s % 2][...]
        act = (0.5 * sc * (1.0 + lax.erf(sc * 0.7071067811865476))).astype(BF16)
        parts = []
        for i in range(e1_per_sub):
            e1 = e * (te // PEER_KEYS) + s * e1_per_sub + i
            coef = jnp.zeros((PEER_KEYS, tl), BF16)
            for h in range(PEER_HEADS):
                lim = lh_scr[h, pl.ds(e1, 1), :].astype(BF16)
                p1 = p1_scr[h, pl.ds(e1, 1), :].astype(BF16)
                coef = coef + jnp.where(r2_scr[h] < lim, p2_scr[h], jnp.zeros((), BF16)) * p1
            parts.append(coef * act[i * PEER_KEYS:(i + 1) * PEER_KEYS, :])
        z = jnp.concatenate(parts, axis=0)
        acc_scr[...] += _mm(vt_ref[:, s * PEER_SUB:(s + 1) * PEER_SUB], z)

    @pl.when(e == pl.num_programs(1) - 1)
    def _():
        y_ref[...] = _rms(x1_ref[...] + acc_scr[...].T, fg_ref[...])


def _peer(qp, h2, x1, wts, tl, te):
    t = h2.shape[0]
    once = pl.Buffered(1)
    keys = lambda: pl.BlockSpec((PEER_KEYS, PEER_HALF), lambda i, e: (0, 0))
    sel = lambda dt=F32: pltpu.VMEM((PEER_HEADS, PEER_KEYS, tl), dt)
    return pl.pallas_call(
        functools.partial(_peer_body, tl=tl, te=te),
        grid=(t // tl, PEER_EXPERTS // te),
        in_specs=[pl.BlockSpec((tl, D_MODEL), lambda i, e: (i, 0), pipeline_mode=once),
                  pl.BlockSpec((tl, D_MODEL), lambda i, e: (i, 0)),
                  pl.BlockSpec((tl, D_MODEL), lambda i, e: (i, 0), pipeline_mode=once),
                  keys(), keys(),
                  pl.BlockSpec((te, D_MODEL), lambda i, e: (e, 0)),
                  pl.BlockSpec((D_MODEL, te), lambda i, e: (0, e)),
                  pl.BlockSpec((1, D_MODEL), lambda i, e: (0, 0))],
        out_specs=pl.BlockSpec((tl, D_MODEL), lambda i, e: (i, 0)),
        out_shape=_sds((t, D_MODEL)),
        scratch_shapes=[sel(), sel(), sel(BF16), sel(BF16), pltpu.VMEM((2, PEER_TOPK, tl), F32),
                        pltpu.VMEM((D_MODEL, tl), F32), pltpu.VMEM((PEER_SUB, tl), F32),
                        pltpu.VMEM((PEER_SUB, tl), F32)],
        compiler_params=_cparams("parallel", "arbitrary"),
        name="peer",
    )(qp, h2, x1, wts["peer_k1"], wts["peer_k2"], wts["peer_u"], wts["peer_vt"], wts["final_norm_g"])


def _pick(n, prefs):
    for p in prefs:
        if n % p == 0:
            return p
    return n


def _group(x3, k_arr, k_cb, v_arr, v_cb, s_gla0, s_gdn0, conv0, wts):
    nb, seq, d = x3.shape
    t = nb * seq
    x2d = x3.reshape(t, d)
    proj, small = _inproj(x2d, wts["norm_mix_g"], wts["w_main"], wts["w_small"],
                          tm=_pick(t, (1024, 512, 256, 128)), tn=1024)
    gla_ch = 16 if seq % 16 == 0 else SUBLANES
    gla_lt = _pick(seq, (256, 128, 64, 32, 16))
    o_a, s_gla = _gla(proj, small, wts, s_gla0, nb, seq, gla_ch, gla_lt)
    o_b, s_gdn, conv_new = _gdn(proj, small, wts, s_gdn0, conv0, nb, seq, _pick(seq, (64, 32, 16)))
    o_c = _xattn(proj, k_arr, k_cb, v_arr, v_cb, nb, seq, _pick(seq, (512, 256, 128)))
    merged = _merge(o_a, o_b, o_c, proj, wts, tm=_pick(t, (256, 128)))
    x1, h2 = _outproj(merged, wts["w_out"], x2d, wts["norm_ffn_g"], tm=_pick(t, (512, 256, 128)))
    qp = _peerq(h2, wts["peer_wq"], tm=_pick(t, (512, 256, 128)))
    y = _peer(qp, h2, x1, wts, tl=_pick(t, (512, 256, 128)), te=1024)
    return y.reshape(nb, seq, d), s_gla, s_gdn, conv_new


def _lanes(vec, start):
    return jnp.zeros((1, SMALL_W), F32).at[0, start:start + vec.shape[0]].set(vec.astype(F32))


def kernel(x_prompt, x_sample, mem_prompt, cache_mem_k, cache_mem_v, state_gla, state_gdn, state_conv, norm_mix_g, norm_mem_g, w_in, w_gla_gate, b_gla_gate, gla_norm_g, gdn_conv_w, gdn_a_log, gdn_dt_bias, gdn_norm_g, w_mem_kv, w_br_gla, w_br_gdn, w_br_mem, b_gates, w_out, norm_ffn_g, peer_wq, peer_k1, peer_k2, peer_u, peer_v, final_norm_g):
    depth = w_in.shape[0]
    assert depth == 1, "the chain below is written for a single layer"
    l = 0
    w_main, w_small = _regroup(w_in.reshape(D_MODEL, IN_COLS), rb=128)
    wts = {
        "norm_mix_g": norm_mix_g[l][None], "w_main": w_main, "w_small": w_small,
        "w_gla_gate": w_gla_gate[l], "b_gla_gate": b_gla_gate[l][None], "gla_norm_g": gla_norm_g[l][None],
        "gdn_conv_w": gdn_conv_w[l], "a_log_lanes": _lanes(gdn_a_log[l], DA_LANE),
        "dt_bias_lanes": _lanes(gdn_dt_bias[l], DA_LANE), "gdn_norm_g": gdn_norm_g[l][None],
        "w_br_gla": w_br_gla[l].astype(BF16), "w_br_gdn": w_br_gdn[l].astype(BF16),
        "w_br_mem": w_br_mem[l].astype(BF16), "b_gates": b_gates[l][None], "w_out": w_out[l].astype(BF16),
        "norm_ffn_g": norm_ffn_g[l][None], "peer_wq": peer_wq[l].astype(BF16),
        "peer_k1": peer_k1[l].astype(BF16), "peer_k2": peer_k2[l].astype(BF16),
        "peer_u": peer_u[l].astype(BF16), "peer_vt": _transpose_cast(peer_v.reshape(PEER_EXPERTS, D_MODEL), tb=512),
        "final_norm_g": final_norm_g[None],
    }
    nb_p, n_mem, _ = mem_prompt.shape
    nb_s = x_sample.shape[0]

    kv = _normmm(mem_prompt.reshape(nb_p * n_mem, D_MODEL), norm_mem_g[l][None], w_mem_kv[l].astype(BF16),
                 tm=_pick(nb_p * n_mem, (512, 256)), tn=1024)
    kv3 = kv.reshape(nb_p, n_mem, 2 * MEM_W)
    y_p, gla_p, gdn_p, conv_p = _group(x_prompt, kv3, 0, kv3, 1, None, None, None, wts)
    mk_p = kv3[:, :, :MEM_W].reshape(nb_p, n_mem, MEM_HEADS, MEM_DH)
    mv_p = kv3[:, :, MEM_W:].reshape(nb_p, n_mem, MEM_HEADS, MEM_DH)

    ck = cache_mem_k.reshape(nb_s, n_mem, MEM_HEADS, MEM_DH)
    cv = cache_mem_v.reshape(nb_s, n_mem, MEM_HEADS, MEM_DH)
    y_s, gla_s, gdn_s, conv_s = _group(x_sample, ck, 0, cv, 0, state_gla[l], state_gdn[l], state_conv[l], wts)

    return (y_p, y_s, gla_p[None], gdn_p[None], conv_p[None], mk_p[None], mv_p[None],
            gla_s[None], gdn_s[None], conv_s[None])
```

```python
import functools

import jax
import jax.numpy as jnp
from jax import lax
from jax.experimental import pallas as pl
from jax.experimental.pallas import tpu as pltpu

F32 = jnp.float32
BF16 = jnp.bfloat16
I32 = jnp.int32
HI = lax.Precision.HIGHEST
NEG_INF = float("-inf")

D_MODEL = 2048
EPS = 1e-6
N_MEM = 256
GLA_HEADS, GLA_DK, GLA_DV, GLA_RANK, GLA_TAU = 4, 128, 256, 16, 16.0
GLA_QK, GLA_V = GLA_HEADS * GLA_DK, GLA_HEADS * GLA_DV
GDN_HEADS, GDN_DK, GDN_DV, GDN_CONV = 8, 128, 128, 4
GDN_QK, GDN_V = GDN_HEADS * GDN_DK, GDN_HEADS * GDN_DV
GDN_CONV_CH = 2 * GDN_QK + GDN_V
MEM_HEADS, MEM_DH = 4, 256
MEM_W = MEM_HEADS * MEM_DH
PEER_KEYS, PEER_HEADS, PEER_HALF, PEER_TOPK = 128, 8, 128, 16
PEER_EXPERTS = PEER_KEYS * PEER_KEYS
PEER_SUB = 512
N_GATES = 3 * D_MODEL
IN_SPLITS = (GLA_QK, GLA_QK, GLA_V, GLA_V, GLA_RANK, GDN_CONV_CH, GDN_V, GDN_HEADS, GDN_HEADS, MEM_W, N_GATES)
IN_NAMES = ("gq", "gk", "gv", "gr", "glr", "dqkv", "dz", "da", "db", "mq", "gates")
N_MAIN = 2 * GLA_QK + 2 * GLA_V + GDN_CONV_CH + GDN_V + MEM_W + N_GATES
SMALL_W = 128
DA_LANE, DB_LANE = GLA_RANK, GLA_RANK + GDN_HEADS
_OFF = {name: sum(IN_SPLITS[:i]) for i, name in enumerate(IN_NAMES)}
IN_COLS = sum(IN_SPLITS)
MAIN_RANGES = ((0, _OFF["glr"]), (_OFF["dqkv"], _OFF["da"]), (_OFF["mq"], IN_COLS))
GLR_TILE = _OFF["glr"]
DAB_TILE = _OFF["da"] - DA_LANE
assert GLR_TILE % SMALL_W == 0 and DAB_TILE % SMALL_W == 0 and sum(b - a for a, b in MAIN_RANGES) == N_MAIN

VMEM_LIMIT_BYTES = 56 * 1024 * 1024
SUBLANES, LANES = 8, 128
MXU_DIM = 256


def _cparams(*sem):
    return pltpu.CompilerParams(dimension_semantics=sem, vmem_limit_bytes=VMEM_LIMIT_BYTES)


def _sds(shape, dtype=F32):
    return jax.ShapeDtypeStruct(shape, dtype)


def _mm(a, b, **kw):
    return jnp.dot(a, b, preferred_element_type=F32, **kw)


def _nt(a, b, **kw):
    return lax.dot_general(a, b, (((1,), (1,)), ((), ())), preferred_element_type=F32, **kw)


def _tn(a, b):
    return lax.dot_general(a, b, (((0,), (0,)), ((), ())), preferred_element_type=F32)


def _softplus(x):
    return jnp.maximum(x, 0.0) + jnp.log1p(jnp.exp(-jnp.abs(x)))


def _silu(x):
    return x * jax.nn.sigmoid(x)


def _rms(x, g):
    return x * lax.rsqrt(jnp.mean(x * x, axis=-1, keepdims=True) + EPS) * g


def _regroup_body(w_ref, o_ref, os_ref):
    w = w_ref[...]
    o_ref[...] = jnp.concatenate([w[:, a:b] for a, b in MAIN_RANGES], axis=1).astype(BF16)
    lane = lax.broadcasted_iota(I32, (w.shape[0], SMALL_W), 1)
    glr_tile = w[:, GLR_TILE:GLR_TILE + SMALL_W]
    dab_tile = w[:, DAB_TILE:DAB_TILE + SMALL_W]
    os_ref[...] = jnp.where(lane < DA_LANE, glr_tile, jnp.where(lane < DB_LANE + GDN_HEADS, dab_tile, 0.0)).astype(BF16)


def _regroup(w_in2d, rb):
    k, n = w_in2d.shape
    return pl.pallas_call(
        _regroup_body,
        grid=(k // rb,),
        in_specs=[pl.BlockSpec((rb, n), lambda i: (i, 0))],
        out_specs=[pl.BlockSpec((rb, N_MAIN), lambda i: (i, 0)), pl.BlockSpec((rb, SMALL_W), lambda i: (i, 0))],
        out_shape=[_sds((k, N_MAIN), BF16), _sds((k, SMALL_W), BF16)],
        compiler_params=_cparams("parallel"),
        name="regroup",
    )(w_in2d)


def _transpose_cast_body(v_ref, o_ref):
    o_ref[...] = v_ref[...].T.astype(BF16)


def _transpose_cast(v, tb):
    n, d = v.shape
    return pl.pallas_call(
        _transpose_cast_body,
        grid=(n // tb,),
        in_specs=[pl.BlockSpec((tb, d), lambda i: (i, 0))],
        out_specs=pl.BlockSpec((d, tb), lambda i: (0, i)),
        out_shape=_sds((d, n), BF16),
        compiler_params=_cparams("parallel"),
        name="vtable_t",
    )(v)


def _inproj_body(x_ref, g_ref, w_ref, ws_ref, o_ref, os_ref, xn_ref):
    @pl.when(pl.program_id(1) == 0)
    def _():
        xn = _rms(x_ref[...], g_ref[...]).astype(BF16)
        xn_ref[...] = xn
        os_ref[...] = _mm(xn, ws_ref[...])

    o_ref[...] = _mm(xn_ref[...], w_ref[...])


def _inproj(x2d, g, w_main, w_small, tm, tn):
    t, k = x2d.shape
    n = w_main.shape[1]
    return pl.pallas_call(
        _inproj_body,
        grid=(t // tm, n // tn),
        in_specs=[pl.BlockSpec((tm, k), lambda i, j: (i, 0)),
                  pl.BlockSpec((1, k), lambda i, j: (0, 0)),
                  pl.BlockSpec((k, tn), lambda i, j: (0, j)),
                  pl.BlockSpec((k, SMALL_W), lambda i, j: (0, 0))],
        out_specs=[pl.BlockSpec((tm, tn), lambda i, j: (i, j)),
                   pl.BlockSpec((tm, SMALL_W), lambda i, j: (i, 0))],
        out_shape=[_sds((t, n)), _sds((t, SMALL_W))],
        scratch_shapes=[pltpu.VMEM((tm, k), BF16)],
        compiler_params=_cparams("parallel", "arbitrary"),
        name="inproj",
    )(x2d, g, w_main, w_small)


def _normmm_body(x_ref, g_ref, w_ref, o_ref, xn_ref):
    @pl.when(pl.program_id(1) == 0)
    def _():
        xn_ref[...] = _rms(x_ref[...], g_ref[...]).astype(BF16)

    o_ref[...] = _mm(xn_ref[...], w_ref[...])


def _normmm(x2d, g, w, tm, tn):
    t, k = x2d.shape
    n = w.shape[1]
    return pl.pallas_call(
        _normmm_body,
        grid=(t // tm, n // tn),
        in_specs=[pl.BlockSpec((tm, k), lambda i, j: (i, 0)),
                  pl.BlockSpec((1, k), lambda i, j: (0, 0)),
                  pl.BlockSpec((k, tn), lambda i, j: (0, j))],
        out_specs=pl.BlockSpec((tm, tn), lambda i, j: (i, j)),
        out_shape=_sds((t, n)),
        scratch_shapes=[pltpu.VMEM((tm, k), BF16)],
        compiler_params=_cparams("parallel", "arbitrary"),
        name="memkv",
    )(x2d, g, w)


def _gla_body(*refs, ch, nc, has_state):
    if has_state:
        (q_ref, k_ref, v_ref, r_ref, sm_ref, wg_ref, bg_ref, ng_ref, tri_ref, s0_ref,
         o_ref, so_ref, st_scr, b_scr) = refs
    else:
        (q_ref, k_ref, v_ref, r_ref, sm_ref, wg_ref, bg_ref, ng_ref, tri_ref,
         o_ref, so_ref, st_scr, b_scr) = refs
    step = pl.program_id(1)

    @pl.when(step == 0)
    def _():
        for h in range(GLA_HEADS):
            if has_state:
                st_scr[h] = s0_ref[0, h].T
            else:
                st_scr[h] = jnp.zeros((GLA_DV, GLA_DK), F32)

    logit = _mm(sm_ref[:, 0:GLA_RANK], wg_ref[...], precision=HI) + bg_ref[...]
    g = (jnp.minimum(logit, 0.0) - jnp.log1p(jnp.exp(-jnp.abs(logit)))) * (1.0 / GLA_TAU)
    b_scr[...] = _mm(tri_ref[...], g, precision=HI)

    rb = (lambda x: x.astype(BF16)) if ch >= 2 * SUBLANES else (lambda x: x)
    row_io = lax.broadcasted_iota(I32, (ch, GLA_DK), 0)
    row_io_v = lax.broadcasted_iota(I32, (ch, GLA_DV), 0)

    def chunk(c, carry):
        rows = pl.ds(pl.multiple_of(c * ch, ch), ch)
        for h in range(GLA_HEADS):
            ksl = slice(h * GLA_DK, (h + 1) * GLA_DK)
            vsl = slice(h * GLA_DV, (h + 1) * GLA_DV)
            qh = q_ref[rows, ksl] * (GLA_DK ** -0.5)
            kh = k_ref[rows, ksl]
            vh = v_ref[rows, vsl]
            bh = b_scr[rows, ksl]
            st = st_scr[h]
            o = _nt(rb(qh * jnp.exp(bh)), rb(st))
            for i in range(ch):
                dec = jnp.exp(jnp.where(row_io <= i, bh[i:i + 1, :] - bh, NEG_INF))
                col = jnp.sum(kh * dec * qh[i:i + 1, :], axis=1, keepdims=True)
                o = jnp.where(row_io_v == i, o + jnp.sum(col * vh, axis=0, keepdims=True), o)
            b_last = bh[ch - 1:ch, :]
            st_scr[h] = jnp.exp(b_last) * st + _tn(rb(vh), rb(kh * jnp.exp(b_last - bh)))
            rh = r_ref[rows, vsl]
            o_ref[rows, vsl] = _rms(o, ng_ref[...]) * _silu(rh)
        return carry

    lax.fori_loop(0, nc, chunk, 0)

    @pl.when(step == pl.num_programs(1) - 1)
    def _():
        for h in range(GLA_HEADS):
            so_ref[0, h] = st_scr[h].T


def _gla(proj, small, wts, s0, nb, seq, ch, lt):
    nsteps = seq // lt
    nc = lt // ch
    t = nb * seq
    rix = lambda b, s: b * nsteps + s
    idx = lax.broadcasted_iota(I32, (lt, lt), 0)
    jdx = lax.broadcasted_iota(I32, (lt, lt), 1)
    tri = ((jdx <= idx) & (idx // ch == jdx // ch)).astype(F32)
    in_specs = [pl.BlockSpec((lt, GLA_QK), lambda b, s: (rix(b, s), 0)),
                pl.BlockSpec((lt, GLA_QK), lambda b, s: (rix(b, s), 1)),
                pl.BlockSpec((lt, GLA_V), lambda b, s: (rix(b, s), 1)),
                pl.BlockSpec((lt, GLA_V), lambda b, s: (rix(b, s), 2)),
                pl.BlockSpec((lt, SMALL_W), lambda b, s: (rix(b, s), 0)),
                pl.BlockSpec((GLA_RANK, GLA_QK), lambda b, s: (0, 0)),
                pl.BlockSpec((1, GLA_QK), lambda b, s: (0, 0)),
                pl.BlockSpec((1, GLA_DV), lambda b, s: (0, 0)),
                pl.BlockSpec((lt, lt), lambda b, s: (0, 0))]
    args = [proj, proj, proj, proj, small, wts["w_gla_gate"], wts["b_gla_gate"], wts["gla_norm_g"], tri]
    if s0 is not None:
        in_specs.append(pl.BlockSpec((1, GLA_HEADS, GLA_DK, GLA_DV), lambda b, s: (b, 0, 0, 0)))
        args.append(s0)
    return pl.pallas_call(
        functools.partial(_gla_body, ch=ch, nc=nc, has_state=s0 is not None),
        grid=(nb, nsteps),
        in_specs=in_specs,
        out_specs=[pl.BlockSpec((lt, GLA_V), lambda b, s: (rix(b, s), 0)),
                   pl.BlockSpec((1, GLA_HEADS, GLA_DK, GLA_DV), lambda b, s: (b, 0, 0, 0))],
        out_shape=[_sds((t, GLA_V)), _sds((nb, GLA_HEADS, GLA_DK, GLA_DV))],
        scratch_shapes=[pltpu.VMEM((GLA_HEADS, GLA_DV, GLA_DK), F32), pltpu.VMEM((lt, GLA_QK), F32)],
        compiler_params=_cparams("parallel", "arbitrary"),
        name="gla",
    )(*args)


def _gdn_body(*refs, c, has_state, nsq, grp):
    if has_state:
        (x_ref, z_ref, sm_ref, cw_ref, al_ref, dtb_ref, ng_ref, s0_ref, c0_ref,
         o_ref, so_ref, co_ref, st_scr, xp_scr) = refs
    else:
        (x_ref, z_ref, sm_ref, cw_ref, al_ref, dtb_ref, ng_ref,
         o_ref, so_ref, co_ref, st_scr, xp_scr) = refs
    step = pl.program_id(1)
    hist = GDN_CONV - 1
    base = SUBLANES - hist

    @pl.when(step == 0)
    def _():
        if has_state:
            st_scr[...] = s0_ref[0]
            xp_scr[base:SUBLANES, :] = c0_ref[0]
        else:
            st_scr[...] = jnp.zeros(st_scr.shape, F32)
            xp_scr[base:SUBLANES, :] = jnp.zeros((hist, GDN_CONV_CH), F32)

    xp_scr[SUBLANES:SUBLANES + c, :] = x_ref[...]
    conv = xp_scr[pl.ds(base, c), :] * cw_ref[0:1, :]
    for w in range(1, GDN_CONV):
        conv = conv + xp_scr[pl.ds(base + w, c), :] * cw_ref[w:w + 1, :]
    tail = xp_scr[pl.ds(c + base, hist), :]
    xp_scr[base:SUBLANES, :] = tail
    co_ref[0] = tail
    act = _silu(conv)

    sm = sm_ref[...]
    gdec = -jnp.exp(al_ref[...]) * _softplus(sm + dtb_ref[...])
    beta = jax.nn.sigmoid(sm)
    ci = lax.broadcasted_iota(I32, (c, c), 0)
    cj = lax.broadcasted_iota(I32, (c, c), 1)
    bcol = _mm((ci >= cj).astype(F32), gdec, precision=HI)
    rb = (lambda x: x.astype(BF16)) if c >= 2 * SUBLANES else (lambda x: x.astype(BF16).astype(F32))

    rows = grp * c
    shift = c.bit_length() - 1
    ii = lax.broadcasted_iota(I32, (rows, rows), 0)
    jj = lax.broadcasted_iota(I32, (rows, rows), 1)
    same = (ii >> shift) == (jj >> shift)
    incl = same & (ii >= jj)
    strict = same & (ii > jj)
    eye = (ii == jj).astype(F32)
    first = ((lax.broadcasted_iota(I32, (SUBLANES, LANES), 0) == 0)
             & (lax.broadcasted_iota(I32, (SUBLANES, LANES), 1) == 0)).astype(F32)
    stack = lambda parts: jnp.concatenate(parts, axis=0) if len(parts) > 1 else parts[0]

    for g in range(GDN_HEADS // grp):
        heads = list(range(g * grp, (g + 1) * grp))
        q_parts, k_parts, v_parts = [], [], []
        for h in heads:
            cq = act[:, h * GDN_DK:(h + 1) * GDN_DK]
            ck = act[:, GDN_QK + h * GDN_DK:GDN_QK + (h + 1) * GDN_DK]
            q_parts.append(cq * lax.rsqrt(jnp.sum(cq * cq, axis=-1, keepdims=True) + EPS) * (GDN_DK ** -0.5))
            k_parts.append(ck * lax.rsqrt(jnp.sum(ck * ck, axis=-1, keepdims=True) + EPS))
            v_parts.append(act[:, 2 * GDN_QK + h * GDN_DV:2 * GDN_QK + (h + 1) * GDN_DV])
        k_all, v_all = stack(k_parts), stack(v_parts)
        bc = stack([bcol[:, DA_LANE + h:DA_LANE + h + 1] for h in heads])
        bt = stack([beta[:, DB_LANE + h:DB_LANE + h + 1] for h in heads])
        br = _nt(first, jnp.broadcast_to(bc, (rows, LANES)), precision=HI)[0:1, :]
        gam = jnp.exp(jnp.where(incl, bc - br, NEG_INF))
        kq = rb(stack(k_parts + q_parts))
        kkqk = _nt(kq, kq[:rows])
        kk, qk = kkqk[:rows], kkqk[rows:]
        p = -(bt * jnp.where(strict, gam, 0.0) * kk)
        ainv = eye + p
        pk = p
        for _ in range(nsq):
            pkb = rb(pk)
            pk = _mm(pkb, pkb)
            ainv = ainv + _mm(rb(ainv), rb(pk))
        ks_parts, qs_parts = [], []
        for n, h in enumerate(heads):
            kq_h = stack([kq[n * c:(n + 1) * c], kq[rows + n * c:rows + (n + 1) * c]])
            kqs = _mm(kq_h, rb(st_scr[h]))
            ks_parts.append(kqs[:c])
            qs_parts.append(kqs[c:])
        eb = jnp.exp(bc)
        u = rb(_mm(rb(ainv), rb(bt * (v_all - eb * stack(ks_parts)))))
        o = eb * stack(qs_parts) + _mm(rb(qk * gam), u)
        for n, h in enumerate(heads):
            r0, r1 = n * c, (n + 1) * c
            sl = slice(h * GDN_DK, (h + 1) * GDN_DK)
            b_last = bc[r1 - 1:r1, :]
            st_scr[h] = jnp.exp(b_last) * st_scr[h] + _tn(rb(k_all[r0:r1] * jnp.exp(b_last - bc[r0:r1])), u[r0:r1])
            o_ref[:, sl] = _rms(o[r0:r1], ng_ref[...]) * _silu(z_ref[:, sl])

    @pl.when(step == pl.num_programs(1) - 1)
    def _():
        so_ref[0] = st_scr[...]


def _gdn(proj, small, wts, s0, conv0, nb, seq, c):
    nsteps = seq // c
    t = nb * seq
    rix = lambda b, s: b * nsteps + s
    nsq = max(c.bit_length() - 2, 0)
    grp = min(GDN_HEADS, MXU_DIM // c)
    in_specs = [pl.BlockSpec((c, GDN_CONV_CH), lambda b, s: (rix(b, s), 1)),
                pl.BlockSpec((c, GDN_V), lambda b, s: (rix(b, s), 6)),
                pl.BlockSpec((c, SMALL_W), lambda b, s: (rix(b, s), 0)),
                pl.BlockSpec((GDN_CONV, GDN_CONV_CH), lambda b, s: (0, 0)),
                pl.BlockSpec((1, SMALL_W), lambda b, s: (0, 0)),
                pl.BlockSpec((1, SMALL_W), lambda b, s: (0, 0)),
                pl.BlockSpec((1, GDN_DV), lambda b, s: (0, 0))]
    args = [proj, proj, small, wts["gdn_conv_w"], wts["a_log_lanes"], wts["dt_bias_lanes"], wts["gdn_norm_g"]]
    if s0 is not None:
        in_specs += [pl.BlockSpec((1, GDN_HEADS, GDN_DK, GDN_DV), lambda b, s: (b, 0, 0, 0)),
                     pl.BlockSpec((1, GDN_CONV - 1, GDN_CONV_CH), lambda b, s: (b, 0, 0))]
        args += [s0, conv0]
    return pl.pallas_call(
        functools.partial(_gdn_body, c=c, has_state=s0 is not None, nsq=nsq, grp=grp),
        grid=(nb, nsteps),
        in_specs=in_specs,
        out_specs=[pl.BlockSpec((c, GDN_V), lambda b, s: (rix(b, s), 0)),
                   pl.BlockSpec((1, GDN_HEADS, GDN_DK, GDN_DV), lambda b, s: (b, 0, 0, 0)),
                   pl.BlockSpec((1, GDN_CONV - 1, GDN_CONV_CH), lambda b, s: (b, 0, 0))],
        out_shape=[_sds((t, GDN_V)), _sds((nb, GDN_HEADS, GDN_DK, GDN_DV)), _sds((nb, GDN_CONV - 1, GDN_CONV_CH))],
        scratch_shapes=[pltpu.VMEM((GDN_HEADS, GDN_DK, GDN_DV), F32),
                        pltpu.VMEM((c + SUBLANES, GDN_CONV_CH), F32)],
        compiler_params=_cparams("parallel", "arbitrary"),
        name="gdn",
    )(*args)


def _xattn_body(q_ref, k_ref, v_ref, o_ref, *, per_head_kv):
    if per_head_kv:
        lt = q_ref.shape[0]
        k_all = k_ref[0].reshape(N_MEM * MEM_HEADS, MEM_DH)
        v_all = v_ref[0].reshape(N_MEM * MEM_HEADS, MEM_DH)
        q_all = jnp.concatenate([q_ref[:, h * MEM_DH:(h + 1) * MEM_DH] for h in range(MEM_HEADS)], axis=0)
        sc = _nt(q_all, k_all) * (MEM_DH ** -0.5)
        row_head = lax.broadcasted_iota(I32, sc.shape, 0) // lt
        col_head = lax.broadcasted_iota(I32, sc.shape, 1) % MEM_HEADS
        sc = jnp.where(row_head == col_head, sc, NEG_INF)
        p = jnp.exp(sc - jnp.max(sc, axis=-1, keepdims=True))
        p = p / jnp.sum(p, axis=-1, keepdims=True)
        o_all = _mm(p, v_all)
        for h in range(MEM_HEADS):
            o_ref[:, h * MEM_DH:(h + 1) * MEM_DH] = o_all[h * lt:(h + 1) * lt]
        return
    for h in range(MEM_HEADS):
        sl = slice(h * MEM_DH, (h + 1) * MEM_DH)
        sc = _nt(q_ref[:, sl], k_ref[0, :, sl]) * (MEM_DH ** -0.5)
        p = jnp.exp(sc - jnp.max(sc, axis=-1, keepdims=True))
        p = p / jnp.sum(p, axis=-1, keepdims=True)
        o_ref[:, sl] = _mm(p, v_ref[0, :, sl])


def _xattn(proj, k_arr, k_cb, v_arr, v_cb, nb, seq, lt):
    nsteps = seq // lt
    t = nb * seq
    per_head_kv = k_arr.ndim == 4
    if per_head_kv:
        kv_spec = lambda cb: pl.BlockSpec((1, N_MEM, MEM_HEADS, MEM_DH), lambda b, s: (b, 0, 0, 0))
    else:
        kv_spec = lambda cb: pl.BlockSpec((1, N_MEM, MEM_W), lambda b, s: (b, 0, cb))
    return pl.pallas_call(
        functools.partial(_xattn_body, per_head_kv=per_head_kv),
        grid=(nb, nsteps),
        in_specs=[pl.BlockSpec((lt, MEM_W), lambda b, s: (b * nsteps + s, 7)),
                  kv_spec(k_cb), kv_spec(v_cb)],
        out_specs=pl.BlockSpec((lt, MEM_W), lambda b, s: (b * nsteps + s, 0)),
        out_shape=_sds((t, MEM_W)),
        compiler_params=_cparams("parallel", "arbitrary"),
        name="xattn",
    )(proj, k_arr, v_arr)


def _merge_body(oa_ref, ob_ref, oc_ref, ga_ref, gb_ref, gc_ref, ba_ref, bb_ref, bc_ref,
                wa_ref, wb_ref, wc_ref, m_ref):
    acc = jax.nn.sigmoid(ga_ref[...] + ba_ref[...]) * _mm(oa_ref[...].astype(BF16), wa_ref[...])
    acc = acc + jax.nn.sigmoid(gb_ref[...] + bb_ref[...]) * _mm(ob_ref[...].astype(BF16), wb_ref[...])
    acc = acc + jax.nn.sigmoid(gc_ref[...] + bc_ref[...]) * _mm(oc_ref[...].astype(BF16), wc_ref[...])
    m_ref[...] = acc.astype(BF16)


def _merge(o_a, o_b, o_c, proj, wts, tm):
    t = o_a.shape[0]
    row = lambda w: pl.BlockSpec((tm, w), lambda i: (i, 0))
    gate = lambda cb: pl.BlockSpec((tm, D_MODEL), lambda i: (i, cb))
    bias = lambda cb: pl.BlockSpec((1, D_MODEL), lambda i: (0, cb))
    wspec = lambda k: pl.BlockSpec((k, D_MODEL), lambda i: (0, 0))
    return pl.pallas_call(
        _merge_body,
        grid=(t // tm,),
        in_specs=[row(GLA_V), row(GDN_V), row(MEM_W), gate(4), gate(5), gate(6), bias(0), bias(1), bias(2),
                  wspec(GLA_V), wspec(GDN_V), wspec(MEM_W)],
        out_specs=pl.BlockSpec((tm, D_MODEL), lambda i: (i, 0)),
        out_shape=_sds((t, D_MODEL), BF16),
        compiler_params=_cparams("parallel"),
        name="merge",
    )(o_a, o_b, o_c, proj, proj, proj, wts["b_gates"], wts["b_gates"], wts["b_gates"],
      wts["w_br_gla"], wts["w_br_gdn"], wts["w_br_mem"])


def _outproj_body(m_ref, w_ref, x_ref, g_ref, x1_ref, h2_ref):
    x1 = x_ref[...] + _mm(m_ref[...], w_ref[...])
    x1_ref[...] = x1
    h2_ref[...] = _rms(x1, g_ref[...]).astype(BF16)


def _outproj(merged, w_out, x2d, g, tm):
    t = x2d.shape[0]
    blk = lambda: pl.BlockSpec((tm, D_MODEL), lambda i: (i, 0))
    return pl.pallas_call(
        _outproj_body,
        grid=(t // tm,),
        in_specs=[blk(), pl.BlockSpec((D_MODEL, D_MODEL), lambda i: (0, 0)), blk(),
                  pl.BlockSpec((1, D_MODEL), lambda i: (0, 0))],
        out_specs=[blk(), blk()],
        out_shape=[_sds((t, D_MODEL)), _sds((t, D_MODEL), BF16)],
        compiler_params=_cparams("parallel"),
        name="outproj",
    )(merged, w_out, x2d, g)


def _peerq_body(h_ref, w_ref, q_ref):
    q_ref[...] = _mm(h_ref[...], w_ref[...]).astype(BF16)


def _peerq(h2, wq, tm):
    t = h2.shape[0]
    n = wq.shape[1]
    return pl.pallas_call(
        _peerq_body,
        grid=(t // tm,),
        in_specs=[pl.BlockSpec((tm, D_MODEL), lambda i: (i, 0)), pl.BlockSpec((D_MODEL, n), lambda i: (0, 0))],
        out_specs=pl.BlockSpec((tm, n), lambda i: (i, 0)),
        out_shape=_sds((t, n), BF16),
        compiler_params=_cparams("parallel"),
        name="peerq",
    )(h2, wq)


def _top16(x, tv_ref, slot):
    io = lax.broadcasted_iota(I32, x.shape, 0)
    rank = jnp.full(x.shape, PEER_TOPK, I32)
    for r in range(PEER_TOPK):
        m = jnp.max(x, axis=0, keepdims=True)
        idx = jnp.min(jnp.where(x == m, io, x.shape[0]), axis=0, keepdims=True)
        hit = io == idx
        rank = jnp.where(hit, r, rank)
        x = jnp.where(hit, NEG_INF, x)
        tv_ref[slot, r:r + 1, :] = m
    return rank


def _peer_select(h, qp_ref, k1_ref, k2_ref, tv_scr, lh_scr, p1_scr, r2_scr, p2_scr, tl):
    base = pl.multiple_of(h * 2 * PEER_HALF, 2 * PEER_HALF)
    s1 = _nt(k1_ref[...], qp_ref[:, pl.ds(base, PEER_HALF)])
    s2 = _nt(k2_ref[...], qp_ref[:, pl.ds(base + PEER_HALF, PEER_HALF)])
    rank1 = _top16(s1, tv_scr, 0)
    rank2 = _top16(s2, tv_scr, 1)
    t1 = tv_scr[0]
    t2 = tv_scr[1]
    a_io = lax.broadcasted_iota(I32, (PEER_TOPK, tl), 0)
    limit = jnp.zeros((PEER_TOPK, tl), I32)
    for _ in range(PEER_TOPK):
        nxt = jnp.broadcast_to(t2[0:1, :], (PEER_TOPK, tl))
        for b in range(1, PEER_TOPK):
            nxt = jnp.where(limit == b, t2[b:b + 1, :], nxt)
        front = jnp.where(limit < PEER_TOPK, t1 + nxt, NEG_INF)
        m = jnp.max(front, axis=0, keepdims=True)
        a_pick = jnp.min(jnp.where(front == m, a_io, PEER_TOPK), axis=0, keepdims=True)
        limit = limit + (a_io == a_pick).astype(I32)
    p1s = jnp.exp(t1 - t1[0:1, :])
    p2s = jnp.exp(t2 - t2[0:1, :])
    inner = jnp.zeros((PEER_TOPK, tl), F32)
    for b in range(PEER_TOPK):
        inner = inner + jnp.where(limit > b, p2s[b:b + 1, :], 0.0)
    inv_z = 1.0 / jnp.sum(p1s * inner, axis=0, keepdims=True)
    lim_e1 = jnp.zeros(rank1.shape, I32)
    for a in range(PEER_TOPK):
        lim_e1 = jnp.where(rank1 == a, limit[a:a + 1, :], lim_e1)
    lh_scr[h] = lim_e1.astype(F32)
    p1_scr[h] = jnp.where(rank1 < PEER_TOPK, jnp.exp(s1 - t1[0:1, :]), 0.0) * inv_z
    r2_scr[h] = rank2.astype(F32).astype(BF16)
    p2_scr[h] = jnp.exp(jnp.minimum(s2 - t2[0:1, :], 0.0)).astype(BF16)


def _peer_body(qp_ref, h2_ref, x1_ref, k1_ref, k2_ref, u_ref, vt_ref, fg_ref, y_ref,
               lh_scr, p1_scr, r2_scr, p2_scr, tv_scr, acc_scr, sca_scr, scb_scr, *, tl, te):
    e = pl.program_id(1)

    @pl.when(e == 0)
    def _():
        acc_scr[...] = jnp.zeros(acc_scr.shape, F32)

        def one_head(h, carry):
            _peer_select(h, qp_ref, k1_ref, k2_ref, tv_scr, lh_scr, p1_scr, r2_scr, p2_scr, tl)
            return carry

        lax.fori_loop(0, PEER_HEADS, one_head, 0)

    e1_per_sub = PEER_SUB // PEER_KEYS
    n_sub = te // PEER_SUB
    sc_bufs = (sca_scr, scb_scr)

    def pre_act(s):
        sc_bufs[s % 2][...] = _nt(u_ref[s * PEER_SUB:(s + 1) * PEER_SUB, :], h2_ref[...])

    pre_act(0)
    for s in range(n_sub):
        if s + 1 < n_sub:
            pre_act(s + 1)
        sc = sc_bufs[s % 2][...]
        act = (0.5 * sc * (1.0 + lax.erf(sc * 0.7071067811865476))).astype(BF16)
        parts = []
        for i in range(e1_per_sub):
            e1 = e * (te // PEER_KEYS) + s * e1_per_sub + i
            coef = jnp.zeros((PEER_KEYS, tl), BF16)
            for h in range(PEER_HEADS):
                lim = lh_scr[h, pl.ds(e1, 1), :].astype(BF16)
                p1 = p1_scr[h, pl.ds(e1, 1), :].astype(BF16)
                coef = coef + jnp.where(r2_scr[h] < lim, p2_scr[h], jnp.zeros((), BF16)) * p1
            parts.append(coef * act[i * PEER_KEYS:(i + 1) * PEER_KEYS, :])
        z = jnp.concatenate(parts, axis=0)
        acc_scr[...] += _mm(vt_ref[:, s * PEER_SUB:(s + 1) * PEER_SUB], z)

    @pl.when(e == pl.num_programs(1) - 1)
    def _():
        y_ref[...] = _rms(x1_ref[...] + acc_scr[...].T, fg_ref[...])


def _peer(qp, h2, x1, wts, tl, te):
    t = h2.shape[0]
    once = pl.Buffered(1)
    keys = lambda: pl.BlockSpec((PEER_KEYS, PEER_HALF), lambda i, e: (0, 0))
    sel = lambda dt=F32: pltpu.VMEM((PEER_HEADS, PEER_KEYS, tl), dt)
    return pl.pallas_call(
        functools.partial(_peer_body, tl=tl, te=te),
        grid=(t // tl, PEER_EXPERTS // te),
        in_specs=[pl.BlockSpec((tl, D_MODEL), lambda i, e: (i, 0), pipeline_mode=once),
                  pl.BlockSpec((tl, D_MODEL), lambda i, e: (i, 0)),
                  pl.BlockSpec((tl, D_MODEL), lambda i, e: (i, 0), pipeline_mode=once),
                  keys(), keys(),
                  pl.BlockSpec((te, D_MODEL), lambda i, e: (e, 0)),
                  pl.BlockSpec((D_MODEL, te), lambda i, e: (0, e)),
                  pl.BlockSpec((1, D_MODEL), lambda i, e: (0, 0))],
        out_specs=pl.BlockSpec((tl, D_MODEL), lambda i, e: (i, 0)),
        out_shape=_sds((t, D_MODEL)),
        scratch_shapes=[sel(), sel(), sel(BF16), sel(BF16), pltpu.VMEM((2, PEER_TOPK, tl), F32),
                        pltpu.VMEM((D_MODEL, tl), F32), pltpu.VMEM((PEER_SUB, tl), F32),
                        pltpu.VMEM((PEER_SUB, tl), F32)],
        compiler_params=_cparams("parallel", "arbitrary"),
        name="peer",
    )(qp, h2, x1, wts["peer_k1"], wts["peer_k2"], wts["peer_u"], wts["peer_vt"], wts["final_norm_g"])


def _pick(n, prefs):
    for p in prefs:
        if n % p == 0:
            return p
    return n


def _group(x3, k_arr, k_cb, v_arr, v_cb, s_gla0, s_gdn0, conv0, wts):
    nb, seq, d = x3.shape
    t = nb * seq
    x2d = x3.reshape(t, d)
    proj, small = _inproj(x2d, wts["norm_mix_g"], wts["w_main"], wts["w_small"],
                          tm=_pick(t, (1024, 512, 256, 128)), tn=1024)
    gla_ch = 16 if seq % 16 == 0 else SUBLANES
    gla_lt = _pick(seq, (256, 128, 64, 32, 16))
    o_a, s_gla = _gla(proj, small, wts, s_gla0, nb, seq, gla_ch, gla_lt)
    o_b, s_gdn, conv_new = _gdn(proj, small, wts, s_gdn0, conv0, nb, seq, _pick(seq, (64, 32, 16)))
    o_c = _xattn(proj, k_arr, k_cb, v_arr, v_cb, nb, seq, _pick(seq, (512, 256, 128)))
    merged = _merge(o_a, o_b, o_c, proj, wts, tm=_pick(t, (256, 128)))
    x1, h2 = _outproj(merged, wts["w_out"], x2d, wts["norm_ffn_g"], tm=_pick(t, (512, 256, 128)))
    qp = _peerq(h2, wts["peer_wq"], tm=_pick(t, (512, 256, 128)))
    y = _peer(qp, h2, x1, wts, tl=_pick(t, (512, 256, 128)), te=1024)
    return y.reshape(nb, seq, d), s_gla, s_gdn, conv_new


def _lanes(vec, start):
    return jnp.zeros((1, SMALL_W), F32).at[0, start:start + vec.shape[0]].set(vec.astype(F32))


def kernel(x_prompt, x_sample, mem_prompt, cache_mem_k, cache_mem_v, state_gla, state_gdn, state_conv, norm_mix_g, norm_mem_g, w_in, w_gla_gate, b_gla_gate, gla_norm_g, gdn_conv_w, gdn_a_log, gdn_dt_bias, gdn_norm_g, w_mem_kv, w_br_gla, w_br_gdn, w_br_mem, b_gates, w_out, norm_ffn_g, peer_wq, peer_k1, peer_k2, peer_u, peer_v, final_norm_g):
    depth = w_in.shape[0]
    assert depth == 1, "the chain below is written for a single layer"
    l = 0
    w_main, w_small = _regroup(w_in.reshape(D_MODEL, IN_COLS), rb=128)
    wts = {
        "norm_mix_g": norm_mix_g[l][None], "w_main": w_main, "w_small": w_small,
        "w_gla_gate": w_gla_gate[l], "b_gla_gate": b_gla_gate[l][None], "gla_norm_g": gla_norm_g[l][None],
        "gdn_conv_w": gdn_conv_w[l], "a_log_lanes": _lanes(gdn_a_log[l], DA_LANE),
        "dt_bias_lanes": _lanes(gdn_dt_bias[l], DA_LANE), "gdn_norm_g": gdn_norm_g[l][None],
        "w_br_gla": w_br_gla[l].astype(BF16), "w_br_gdn": w_br_gdn[l].astype(BF16),
        "w_br_mem": w_br_mem[l].astype(BF16), "b_gates": b_gates[l][None], "w_out": w_out[l].astype(BF16),
        "norm_ffn_g": norm_ffn_g[l][None], "peer_wq": peer_wq[l].astype(BF16),
        "peer_k1": peer_k1[l].astype(BF16), "peer_k2": peer_k2[l].astype(BF16),
        "peer_u": peer_u[l].astype(BF16), "peer_vt": _transpose_cast(peer_v.reshape(PEER_EXPERTS, D_MODEL), tb=512),
        "final_norm_g": final_norm_g[None],
    }
    nb_p, n_mem, _ = mem_prompt.shape
    nb_s = x_sample.shape[0]

    kv = _normmm(mem_prompt.reshape(nb_p * n_mem, D_MODEL), norm_mem_g[l][None], w_mem_kv[l].astype(BF16),
                 tm=_pick(nb_p * n_mem, (512, 256)), tn=1024)
    kv3 = kv.reshape(nb_p, n_mem, 2 * MEM_W)
    y_p, gla_p, gdn_p, conv_p = _group(x_prompt, kv3, 0, kv3, 1, None, None, None, wts)
    mk_p = kv3[:, :, :MEM_W].reshape(nb_p, n_mem, MEM_HEADS, MEM_DH)
    mv_p = kv3[:, :, MEM_W:].reshape(nb_p, n_mem, MEM_HEADS, MEM_DH)

    ck = cache_mem_k.reshape(nb_s, n_mem, MEM_HEADS, MEM_DH)
    cv = cache_mem_v.reshape(nb_s, n_mem, MEM_HEADS, MEM_DH)
    y_s, gla_s, gdn_s, conv_s = _group(x_sample, ck, 0, cv, 0, state_gla[l], state_gdn[l], state_conv[l], wts)

    return (y_p, y_s, gla_p[None], gdn_p[None], conv_p[None], mk_p[None], mv_p[None],
            gla_s[None], gdn_s[None], conv_s[None])
```

```python
import functools

import jax
import jax.numpy as jnp
from jax import lax
from jax.experimental import pallas as pl
from jax.experimental.pallas import tpu as pltpu

F32 = jnp.float32
BF16 = jnp.bfloat16
I32 = jnp.int32
HI = lax.Precision.HIGHEST
NEG_INF = float("-inf")

D_MODEL = 2048
EPS = 1e-6
N_MEM = 256
GLA_HEADS, GLA_DK, GLA_DV, GLA_RANK, GLA_TAU = 4, 128, 256, 16, 16.0
GLA_QK, GLA_V = GLA_HEADS * GLA_DK, GLA_HEADS * GLA_DV
GDN_HEADS, GDN_DK, GDN_DV, GDN_CONV = 8, 128, 128, 4
GDN_QK, GDN_V = GDN_HEADS * GDN_DK, GDN_HEADS * GDN_DV
GDN_CONV_CH = 2 * GDN_QK + GDN_V
MEM_HEADS, MEM_DH = 4, 256
MEM_W = MEM_HEADS * MEM_DH
PEER_KEYS, PEER_HEADS, PEER_HALF, PEER_TOPK = 128, 8, 128, 16
PEER_EXPERTS = PEER_KEYS * PEER_KEYS
PEER_SUB = 512
N_GATES = 3 * D_MODEL
IN_SPLITS = (GLA_QK, GLA_QK, GLA_V, GLA_V, GLA_RANK, GDN_CONV_CH, GDN_V, GDN_HEADS, GDN_HEADS, MEM_W, N_GATES)
IN_NAMES = ("gq", "gk", "gv", "gr", "glr", "dqkv", "dz", "da", "db", "mq", "gates")
N_MAIN = 2 * GLA_QK + 2 * GLA_V + GDN_CONV_CH + GDN_V + MEM_W + N_GATES
SMALL_W = 128
DA_LANE, DB_LANE = GLA_RANK, GLA_RANK + GDN_HEADS
_OFF = {name: sum(IN_SPLITS[:i]) for i, name in enumerate(IN_NAMES)}
IN_COLS = sum(IN_SPLITS)
MAIN_RANGES = ((0, _OFF["glr"]), (_OFF["dqkv"], _OFF["da"]), (_OFF["mq"], IN_COLS))
GLR_TILE = _OFF["glr"]
DAB_TILE = _OFF["da"] - DA_LANE
assert GLR_TILE % SMALL_W == 0 and DAB_TILE % SMALL_W == 0 and sum(b - a for a, b in MAIN_RANGES) == N_MAIN

VMEM_LIMIT_BYTES = 56 * 1024 * 1024
SUBLANES, LANES = 8, 128
MXU_DIM = 256


def _cparams(*sem):
    return pltpu.CompilerParams(dimension_semantics=sem, vmem_limit_bytes=VMEM_LIMIT_BYTES)


def _sds(shape, dtype=F32):
    return jax.ShapeDtypeStruct(shape, dtype)


def _mm(a, b, **kw):
    return jnp.dot(a, b, preferred_element_type=F32, **kw)


def _nt(a, b, **kw):
    return lax.dot_general(a, b, (((1,), (1,)), ((), ())), preferred_element_type=F32, **kw)


def _tn(a, b):
    return lax.dot_general(a, b, (((0,), (0,)), ((), ())), preferred_element_type=F32)


def _softplus(x):
    return jnp.maximum(x, 0.0) + jnp.log1p(jnp.exp(-jnp.abs(x)))


def _silu(x):
    return x * jax.nn.sigmoid(x)


def _rms(x, g):
    return x * lax.rsqrt(jnp.mean(x * x, axis=-1, keepdims=True) + EPS) * g


def _regroup_body(w_ref, o_ref, os_ref):
    w = w_ref[...]
    o_ref[...] = jnp.concatenate([w[:, a:b] for a, b in MAIN_RANGES], axis=1).astype(BF16)
    lane = lax.broadcasted_iota(I32, (w.shape[0], SMALL_W), 1)
    glr_tile = w[:, GLR_TILE:GLR_TILE + SMALL_W]
    dab_tile = w[:, DAB_TILE:DAB_TILE + SMALL_W]
    os_ref[...] = jnp.where(lane < DA_LANE, glr_tile, jnp.where(lane < DB_LANE + GDN_HEADS, dab_tile, 0.0)).astype(BF16)


def _regroup(w_in2d, rb):
    k, n = w_in2d.shape
    return pl.pallas_call(
        _regroup_body,
        grid=(k // rb,),
        in_specs=[pl.BlockSpec((rb, n), lambda i: (i, 0))],
        out_specs=[pl.BlockSpec((rb, N_MAIN), lambda i: (i, 0)), pl.BlockSpec((rb, SMALL_W), lambda i: (i, 0))],
        out_shape=[_sds((k, N_MAIN), BF16), _sds((k, SMALL_W), BF16)],
        compiler_params=_cparams("parallel"),
        name="regroup",
    )(w_in2d)


def _transpose_cast_body(v_ref, o_ref):
    o_ref[...] = v_ref[...].T.astype(BF16)


def _transpose_cast(v, tb):
    n, d = v.shape
    return pl.pallas_call(
        _transpose_cast_body,
        grid=(n // tb,),
        in_specs=[pl.BlockSpec((tb, d), lambda i: (i, 0))],
        out_specs=pl.BlockSpec((d, tb), lambda i: (0, i)),
        out_shape=_sds((d, n), BF16),
        compiler_params=_cparams("parallel"),
        name="vtable_t",
    )(v)


def _inproj_body(x_ref, g_ref, w_ref, ws_ref, o_ref, os_ref, xn_ref):
    @pl.when(pl.program_id(1) == 0)
    def _():
        xn = _rms(x_ref[...], g_ref[...]).astype(BF16)
        xn_ref[...] = xn
        os_ref[...] = _mm(xn, ws_ref[...])

    o_ref[...] = _mm(xn_ref[...], w_ref[...])


def _inproj(x2d, g, w_main, w_small, tm, tn):
    t, k = x2d.shape
    n = w_main.shape[1]
    return pl.pallas_call(
        _inproj_body,
        grid=(t // tm, n // tn),
        in_specs=[pl.BlockSpec((tm, k), lambda i, j: (i, 0)),
                  pl.BlockSpec((1, k), lambda i, j: (0, 0)),
                  pl.BlockSpec((k, tn), lambda i, j: (0, j)),
                  pl.BlockSpec((k, SMALL_W), lambda i, j: (0, 0))],
        out_specs=[pl.BlockSpec((tm, tn), lambda i, j: (i, j)),
                   pl.BlockSpec((tm, SMALL_W), lambda i, j: (i, 0))],
        out_shape=[_sds((t, n)), _sds((t, SMALL_W))],
        scratch_shapes=[pltpu.VMEM((tm, k), BF16)],
        compiler_params=_cparams("parallel", "arbitrary"),
        name="inproj",
    )(x2d, g, w_main, w_small)


def _normmm_body(x_ref, g_ref, w_ref, o_ref, xn_ref):
    @pl.when(pl.program_id(1) == 0)
    def _():
        xn_ref[...] = _rms(x_ref[...], g_ref[...]).astype(BF16)

    o_ref[...] = _mm(xn_ref[...], w_ref[...])


def _normmm(x2d, g, w, tm, tn):
    t, k = x2d.shape
    n = w.shape[1]
    return pl.pallas_call(
        _normmm_body,
        grid=(t // tm, n // tn),
        in_specs=[pl.BlockSpec((tm, k), lambda i, j: (i, 0)),
                  pl.BlockSpec((1, k), lambda i, j: (0, 0)),
                  pl.BlockSpec((k, tn), lambda i, j: (0, j))],
        out_specs=pl.BlockSpec((tm, tn), lambda i, j: (i, j)),
        out_shape=_sds((t, n)),
        scratch_shapes=[pltpu.VMEM((tm, k), BF16)],
        compiler_params=_cparams("parallel", "arbitrary"),
        name="memkv",
    )(x2d, g, w)


def _gla_body(*refs, ch, nc, has_state):
    if has_state:
        (q_ref, k_ref, v_ref, r_ref, sm_ref, wg_ref, bg_ref, ng_ref, tri_ref, s0_ref,
         o_ref, so_ref, st_scr, b_scr) = refs
    else:
        (q_ref, k_ref, v_ref, r_ref, sm_ref, wg_ref, bg_ref, ng_ref, tri_ref,
         o_ref, so_ref, st_scr, b_scr) = refs
    step = pl.program_id(1)

    @pl.when(step == 0)
    def _():
        for h in range(GLA_HEADS):
            if has_state:
                st_scr[h] = s0_ref[0, h].T
            else:
                st_scr[h] = jnp.zeros((GLA_DV, GLA_DK), F32)

    logit = _mm(sm_ref[:, 0:GLA_RANK], wg_ref[...], precision=HI) + bg_ref[...]
    g = (jnp.minimum(logit, 0.0) - jnp.log1p(jnp.exp(-jnp.abs(logit)))) * (1.0 / GLA_TAU)
    b_scr[...] = _mm(tri_ref[...], g, precision=HI)

    rb = (lambda x: x.astype(BF16)) if ch >= 2 * SUBLANES else (lambda x: x)
    row_io = lax.broadcasted_iota(I32, (ch, GLA_DK), 0)
    row_io_v = lax.broadcasted_iota(I32, (ch, GLA_DV), 0)

    def chunk(c, carry):
        rows = pl.ds(pl.multiple_of(c * ch, ch), ch)
        for h in range(GLA_HEADS):
            ksl = slice(h * GLA_DK, (h + 1) * GLA_DK)
            vsl = slice(h * GLA_DV, (h + 1) * GLA_DV)
            qh = q_ref[rows, ksl] * (GLA_DK ** -0.5)
            kh = k_ref[rows, ksl]
            vh = v_ref[rows, vsl]
            bh = b_scr[rows, ksl]
            st = st_scr[h]
            o = _nt(rb(qh * jnp.exp(bh)), rb(st))
            for i in range(ch):
                dec = jnp.exp(jnp.where(row_io <= i, bh[i:i + 1, :] - bh, NEG_INF))
                col = jnp.sum(kh * dec * qh[i:i + 1, :], axis=1, keepdims=True)
                o = jnp.where(row_io_v == i, o + jnp.sum(col * vh, axis=0, keepdims=True), o)
            b_last = bh[ch - 1:ch, :]
            st_scr[h] = jnp.exp(b_last) * st + _tn(rb(vh), rb(kh * jnp.exp(b_last - bh)))
            rh = r_ref[rows, vsl]
            o_ref[rows, vsl] = _rms(o, ng_ref[...]) * _silu(rh)
        return carry

    lax.fori_loop(0, nc, chunk, 0)

    @pl.when(step == pl.num_programs(1) - 1)
    def _():
        for h in range(GLA_HEADS):
            so_ref[0, h] = st_scr[h].T


def _gla(proj, small, wts, s0, nb, seq, ch, lt):
    nsteps = seq // lt
    nc = lt // ch
    t = nb * seq
    rix = lambda b, s: b * nsteps + s
    idx = lax.broadcasted_iota(I32, (lt, lt), 0)
    jdx = lax.broadcasted_iota(I32, (lt, lt), 1)
    tri = ((jdx <= idx) & (idx // ch == jdx // ch)).astype(F32)
    in_specs = [pl.BlockSpec((lt, GLA_QK), lambda b, s: (rix(b, s), 0)),
                pl.BlockSpec((lt, GLA_QK), lambda b, s: (rix(b, s), 1)),
                pl.BlockSpec((lt, GLA_V), lambda b, s: (rix(b, s), 1)),
                pl.BlockSpec((lt, GLA_V), lambda b, s: (rix(b, s), 2)),
                pl.BlockSpec((lt, SMALL_W), lambda b, s: (rix(b, s), 0)),
                pl.BlockSpec((GLA_RANK, GLA_QK), lambda b, s: (0, 0)),
                pl.BlockSpec((1, GLA_QK), lambda b, s: (0, 0)),
                pl.BlockSpec((1, GLA_DV), lambda b, s: (0, 0)),
                pl.BlockSpec((lt, lt), lambda b, s: (0, 0))]
    args = [proj, proj, proj, proj, small, wts["w_gla_gate"], wts["b_gla_gate"], wts["gla_norm_g"], tri]
    if s0 is not None:
        in_specs.append(pl.BlockSpec((1, GLA_HEADS, GLA_DK, GLA_DV), lambda b, s: (b, 0, 0, 0)))
        args.append(s0)
    return pl.pallas_call(
        functools.partial(_gla_body, ch=ch, nc=nc, has_state=s0 is not None),
        grid=(nb, nsteps),
        in_specs=in_specs,
        out_specs=[pl.BlockSpec((lt, GLA_V), lambda b, s: (rix(b, s), 0)),
                   pl.BlockSpec((1, GLA_HEADS, GLA_DK, GLA_DV), lambda b, s: (b, 0, 0, 0))],
        out_shape=[_sds((t, GLA_V)), _sds((nb, GLA_HEADS, GLA_DK, GLA_DV))],
        scratch_shapes=[pltpu.VMEM((GLA_HEADS, GLA_DV, GLA_DK), F32), pltpu.VMEM((lt, GLA_QK), F32)],
        compiler_params=_cparams("parallel", "arbitrary"),
        name="gla",
    )(*args)


def _gdn_body(*refs, c, has_state, nsq, grp):
    if has_state:
        (x_ref, z_ref, sm_ref, cw_ref, al_ref, dtb_ref, ng_ref, s0_ref, c0_ref,
         o_ref, so_ref, co_ref, st_scr, xp_scr) = refs
    else:
        (x_ref, z_ref, sm_ref, cw_ref, al_ref, dtb_ref, ng_ref,
         o_ref, so_ref, co_ref, st_scr, xp_scr) = refs
    step = pl.program_id(1)
    hist = GDN_CONV - 1
    base = SUBLANES - hist

    @pl.when(step == 0)
    def _():
        if has_state:
            st_scr[...] = s0_ref[0]
            xp_scr[base:SUBLANES, :] = c0_ref[0]
        else:
            st_scr[...] = jnp.zeros(st_scr.shape, F32)
            xp_scr[base:SUBLANES, :] = jnp.zeros((hist, GDN_CONV_CH), F32)

    xp_scr[SUBLANES:SUBLANES + c, :] = x_ref[...]
    conv = xp_scr[pl.ds(base, c), :] * cw_ref[0:1, :]
    for w in range(1, GDN_CONV):
        conv = conv + xp_scr[pl.ds(base + w, c), :] * cw_ref[w:w + 1, :]
    tail = xp_scr[pl.ds(c + base, hist), :]
    xp_scr[base:SUBLANES, :] = tail
    co_ref[0] = tail
    act = _silu(conv)

    sm = sm_ref[...]
    gdec = -jnp.exp(al_ref[...]) * _softplus(sm + dtb_ref[...])
    beta = jax.nn.sigmoid(sm)
    ci = lax.broadcasted_iota(I32, (c, c), 0)
    cj = lax.broadcasted_iota(I32, (c, c), 1)
    bcol = _mm((ci >= cj).astype(F32), gdec, precision=HI)
    rb = (lambda x: x.astype(BF16)) if c >= 2 * SUBLANES else (lambda x: x.astype(BF16).astype(F32))

    rows = grp * c
    shift = c.bit_length() - 1
    ii = lax.broadcasted_iota(I32, (rows, rows), 0)
    jj = lax.broadcasted_iota(I32, (rows, rows), 1)
    same = (ii >> shift) == (jj >> shift)
    incl = same & (ii >= jj)
    strict = same & (ii > jj)
    eye = (ii == jj).astype(F32)
    first = ((lax.broadcasted_iota(I32, (SUBLANES, LANES), 0) == 0)
             & (lax.broadcasted_iota(I32, (SUBLANES, LANES), 1) == 0)).astype(F32)
    stack = lambda parts: jnp.concatenate(parts, axis=0) if len(parts) > 1 else parts[0]

    for g in range(GDN_HEADS // grp):
        heads = list(range(g * grp, (g + 1) * grp))
        q_parts, k_parts, v_parts = [], [], []
        for h in heads:
            cq = act[:, h * GDN_DK:(h + 1) * GDN_DK]
            ck = act[:, GDN_QK + h * GDN_DK:GDN_QK + (h + 1) * GDN_DK]
            q_parts.append(cq * lax.rsqrt(jnp.sum(cq * cq, axis=-1, keepdims=True) + EPS) * (GDN_DK ** -0.5))
            k_parts.append(ck * lax.rsqrt(jnp.sum(ck * ck, axis=-1, keepdims=True) + EPS))
            v_parts.append(act[:, 2 * GDN_QK + h * GDN_DV:2 * GDN_QK + (h + 1) * GDN_DV])
        k_all, v_all = stack(k_parts), stack(v_parts)
        bc = stack([bcol[:, DA_LANE + h:DA_LANE + h + 1] for h in heads])
        bt = stack([beta[:, DB_LANE + h:DB_LANE + h + 1] for h in heads])
        br = _nt(first, jnp.broadcast_to(bc, (rows, LANES)), precision=HI)[0:1, :]
        gam = jnp.exp(jnp.where(incl, bc - br, NEG_INF))
        kq = rb(stack(k_parts + q_parts))
        kkqk = _nt(kq, kq[:rows])
        kk, qk = kkqk[:rows], kkqk[rows:]
        p = -(bt * jnp.where(strict, gam, 0.0) * kk)
        ainv = eye + p
        pk = p
        for _ in range(nsq):
            pkb = rb(pk)
            pk = _mm(pkb, pkb)
            ainv = ainv + _mm(rb(ainv), rb(pk))
        ks_parts, qs_parts = [], []
        for n, h in enumerate(heads):
            kq_h = stack([kq[n * c:(n + 1) * c], kq[rows + n * c:rows + (n + 1) * c]])
            kqs = _mm(kq_h, rb(st_scr[h]))
            ks_parts.append(kqs[:c])
            qs_parts.append(kqs[c:])
        eb = jnp.exp(bc)
        u = rb(_mm(rb(ainv), rb(bt * (v_all - eb * stack(ks_parts)))))
        o = eb * stack(qs_parts) + _mm(rb(qk * gam), u)
        for n, h in enumerate(heads):
            r0, r1 = n * c, (n + 1) * c
            sl = slice(h * GDN_DK, (h + 1) * GDN_DK)
            b_last = bc[r1 - 1:r1, :]
            st_scr[h] = jnp.exp(b_last) * st_scr[h] + _tn(rb(k_all[r0:r1] * jnp.exp(b_last - bc[r0:r1])), u[r0:r1])
            o_ref[:, sl] = _rms(o[r0:r1], ng_ref[...]) * _silu(z_ref[:, sl])

    @pl.when(step == pl.num_programs(1) - 1)
    def _():
        so_ref[0] = st_scr[...]


def _gdn(proj, small, wts, s0, conv0, nb, seq, c):
    nsteps = seq // c
    t = nb * seq
    rix = lambda b, s: b * nsteps + s
    nsq = max(c.bit_length() - 2, 0)
    grp = min(GDN_HEADS, MXU_DIM // c)
    in_specs = [pl.BlockSpec((c, GDN_CONV_CH), lambda b, s: (rix(b, s), 1)),
                pl.BlockSpec((c, GDN_V), lambda b, s: (rix(b, s), 6)),
                pl.BlockSpec((c, SMALL_W), lambda b, s: (rix(b, s), 0)),
                pl.BlockSpec((GDN_CONV, GDN_CONV_CH), lambda b, s: (0, 0)),
                pl.BlockSpec((1, SMALL_W), lambda b, s: (0, 0)),
                pl.BlockSpec((1, SMALL_W), lambda b, s: (0, 0)),
                pl.BlockSpec((1, GDN_DV), lambda b, s: (0, 0))]
    args = [proj, proj, small, wts["gdn_conv_w"], wts["a_log_lanes"], wts["dt_bias_lanes"], wts["gdn_norm_g"]]
    if s0 is not None:
        in_specs += [pl.BlockSpec((1, GDN_HEADS, GDN_DK, GDN_DV), lambda b, s: (b, 0, 0, 0)),
                     pl.BlockSpec((1, GDN_CONV - 1, GDN_CONV_CH), lambda b, s: (b, 0, 0))]
        args += [s0, conv0]
    return pl.pallas_call(
        functools.partial(_gdn_body, c=c, has_state=s0 is not None, nsq=nsq, grp=grp),
        grid=(nb, nsteps),
        in_specs=in_specs,
        out_specs=[pl.BlockSpec((c, GDN_V), lambda b, s: (rix(b, s), 0)),
                   pl.BlockSpec((1, GDN_HEADS, GDN_DK, GDN_DV), lambda b, s: (b, 0, 0, 0)),
                   pl.BlockSpec((1, GDN_CONV - 1, GDN_CONV_CH), lambda b, s: (b, 0, 0))],
        out_shape=[_sds((t, GDN_V)), _sds((nb, GDN_HEADS, GDN_DK, GDN_DV)), _sds((nb, GDN_CONV - 1, GDN_CONV_CH))],
        scratch_shapes=[pltpu.VMEM((GDN_HEADS, GDN_DK, GDN_DV), F32),
                        pltpu.VMEM((c + SUBLANES, GDN_CONV_CH), F32)],
        compiler_params=_cparams("parallel", "arbitrary"),
        name="gdn",
    )(*args)


def _xattn_body(q_ref, k_ref, v_ref, o_ref, *, per_head_kv):
    if per_head_kv:
        nbb = k_ref.shape[0]
        lt = q_ref.shape[0] // nbb
        for bb in range(nbb):
            r0 = bb * lt
            k_all = k_ref[bb].reshape(N_MEM * MEM_HEADS, MEM_DH)
            v_all = v_ref[bb].reshape(N_MEM * MEM_HEADS, MEM_DH)
            q_all = jnp.concatenate([q_ref[r0:r0 + lt, h * MEM_DH:(h + 1) * MEM_DH] for h in range(MEM_HEADS)],
                                    axis=0)
            sc = _nt(q_all, k_all) * (MEM_DH ** -0.5)
            row_head = lax.broadcasted_iota(I32, sc.shape, 0) // lt
            col_head = lax.broadcasted_iota(I32, sc.shape, 1) % MEM_HEADS
            sc = jnp.where(row_head == col_head, sc, NEG_INF)
            p = jnp.exp(sc - jnp.max(sc, axis=-1, keepdims=True))
            p = p / jnp.sum(p, axis=-1, keepdims=True)
            o_all = _mm(p, v_all)
            for h in range(MEM_HEADS):
                o_ref[r0:r0 + lt, h * MEM_DH:(h + 1) * MEM_DH] = o_all[h * lt:(h + 1) * lt]
        return
    for h in range(MEM_HEADS):
        sl = slice(h * MEM_DH, (h + 1) * MEM_DH)
        sc = _nt(q_ref[:, sl], k_ref[0, :, sl]) * (MEM_DH ** -0.5)
        p = jnp.exp(sc - jnp.max(sc, axis=-1, keepdims=True))
        p = p / jnp.sum(p, axis=-1, keepdims=True)
        o_ref[:, sl] = _mm(p, v_ref[0, :, sl])


def _xattn(proj, k_arr, k_cb, v_arr, v_cb, nb, seq, lt):
    nsteps = seq // lt
    t = nb * seq
    per_head_kv = k_arr.ndim == 4
    nbb = 1
    if per_head_kv:
        nbb = _pick(nb, (4, 2)) if nsteps == 1 else 1
        kv_spec = lambda cb: pl.BlockSpec((nbb, N_MEM, MEM_HEADS, MEM_DH), lambda b, s: (b, 0, 0, 0))
    else:
        kv_spec = lambda cb: pl.BlockSpec((1, N_MEM, MEM_W), lambda b, s: (b, 0, cb))
    return pl.pallas_call(
        functools.partial(_xattn_body, per_head_kv=per_head_kv),
        grid=(nb // nbb, nsteps),
        in_specs=[pl.BlockSpec((nbb * lt, MEM_W), lambda b, s: (b * nsteps + s, 7)),
                  kv_spec(k_cb), kv_spec(v_cb)],
        out_specs=pl.BlockSpec((nbb * lt, MEM_W), lambda b, s: (b * nsteps + s, 0)),
        out_shape=_sds((t, MEM_W)),
        compiler_params=_cparams("parallel", "arbitrary"),
        name="xattn",
    )(proj, k_arr, v_arr)


def _merge_body(oa_ref, ob_ref, oc_ref, ga_ref, gb_ref, gc_ref, ba_ref, bb_ref, bc_ref,
                wa_ref, wb_ref, wc_ref, m_ref):
    acc = jax.nn.sigmoid(ga_ref[...] + ba_ref[...]) * _mm(oa_ref[...].astype(BF16), wa_ref[...])
    acc = acc + jax.nn.sigmoid(gb_ref[...] + bb_ref[...]) * _mm(ob_ref[...].astype(BF16), wb_ref[...])
    acc = acc + jax.nn.sigmoid(gc_ref[...] + bc_ref[...]) * _mm(oc_ref[...].astype(BF16), wc_ref[...])
    m_ref[...] = acc.astype(BF16)


def _merge(o_a, o_b, o_c, proj, wts, tm):
    t = o_a.shape[0]
    row = lambda w: pl.BlockSpec((tm, w), lambda i: (i, 0))
    gate = lambda cb: pl.BlockSpec((tm, D_MODEL), lambda i: (i, cb))
    bias = lambda cb: pl.BlockSpec((1, D_MODEL), lambda i: (0, cb))
    wspec = lambda k: pl.BlockSpec((k, D_MODEL), lambda i: (0, 0))
    return pl.pallas_call(
        _merge_body,
        grid=(t // tm,),
        in_specs=[row(GLA_V), row(GDN_V), row(MEM_W), gate(4), gate(5), gate(6), bias(0), bias(1), bias(2),
                  wspec(GLA_V), wspec(GDN_V), wspec(MEM_W)],
        out_specs=pl.BlockSpec((tm, D_MODEL), lambda i: (i, 0)),
        out_shape=_sds((t, D_MODEL), BF16),
        compiler_params=_cparams("parallel"),
        name="merge",
    )(o_a, o_b, o_c, proj, proj, proj, wts["b_gates"], wts["b_gates"], wts["b_gates"],
      wts["w_br_gla"], wts["w_br_gdn"], wts["w_br_mem"])


def _outproj_body(m_ref, w_ref, x_ref, g_ref, x1_ref, h2_ref):
    x1 = x_ref[...] + _mm(m_ref[...], w_ref[...])
    x1_ref[...] = x1
    h2_ref[...] = _rms(x1, g_ref[...]).astype(BF16)


def _outproj(merged, w_out, x2d, g, tm):
    t = x2d.shape[0]
    blk = lambda: pl.BlockSpec((tm, D_MODEL), lambda i: (i, 0))
    return pl.pallas_call(
        _outproj_body,
        grid=(t // tm,),
        in_specs=[blk(), pl.BlockSpec((D_MODEL, D_MODEL), lambda i: (0, 0)), blk(),
                  pl.BlockSpec((1, D_MODEL), lambda i: (0, 0))],
        out_specs=[blk(), blk()],
        out_shape=[_sds((t, D_MODEL)), _sds((t, D_MODEL), BF16)],
        compiler_params=_cparams("parallel"),
        name="outproj",
    )(merged, w_out, x2d, g)


def _peerq_body(h_ref, w_ref, q_ref):
    q_ref[...] = _mm(h_ref[...], w_ref[...]).astype(BF16)


def _peerq(h2, wq, tm):
    t = h2.shape[0]
    n = wq.shape[1]
    return pl.pallas_call(
        _peerq_body,
        grid=(t // tm,),
        in_specs=[pl.BlockSpec((tm, D_MODEL), lambda i: (i, 0)), pl.BlockSpec((D_MODEL, n), lambda i: (0, 0))],
        out_specs=pl.BlockSpec((tm, n), lambda i: (i, 0)),
        out_shape=_sds((t, n), BF16),
        compiler_params=_cparams("parallel"),
        name="peerq",
    )(h2, wq)


def _top16(x, tv_ref, slot):
    io = lax.broadcasted_iota(I32, x.shape, 0)
    rank = jnp.full(x.shape, PEER_TOPK, I32)
    for r in range(PEER_TOPK):
        m = jnp.max(x, axis=0, keepdims=True)
        idx = jnp.min(jnp.where(x == m, io, x.shape[0]), axis=0, keepdims=True)
        hit = io == idx
        rank = jnp.where(hit, r, rank)
        x = jnp.where(hit, NEG_INF, x)
        tv_ref[slot, r:r + 1, :] = m
    return rank


def _peer_select(h, qp_ref, k1_ref, k2_ref, tv_scr, lh_scr, p1_scr, r2_scr, p2_scr, tl):
    base = pl.multiple_of(h * 2 * PEER_HALF, 2 * PEER_HALF)
    s1 = _nt(k1_ref[...], qp_ref[:, pl.ds(base, PEER_HALF)])
    s2 = _nt(k2_ref[...], qp_ref[:, pl.ds(base + PEER_HALF, PEER_HALF)])
    rank1 = _top16(s1, tv_scr, 0)
    rank2 = _top16(s2, tv_scr, 1)
    t1 = tv_scr[0]
    t2 = tv_scr[1]
    a_io = lax.broadcasted_iota(I32, (PEER_TOPK, tl), 0)
    limit = jnp.zeros((PEER_TOPK, tl), I32)
    for _ in range(PEER_TOPK):
        nxt = jnp.broadcast_to(t2[0:1, :], (PEER_TOPK, tl))
        for b in range(1, PEER_TOPK):
            nxt = jnp.where(limit == b, t2[b:b + 1, :], nxt)
        front = jnp.where(limit < PEER_TOPK, t1 + nxt, NEG_INF)
        m = jnp.max(front, axis=0, keepdims=True)
        a_pick = jnp.min(jnp.where(front == m, a_io, PEER_TOPK), axis=0, keepdims=True)
        limit = limit + (a_io == a_pick).astype(I32)
    p1s = jnp.exp(t1 - t1[0:1, :])
    p2s = jnp.exp(t2 - t2[0:1, :])
    inner = jnp.zeros((PEER_TOPK, tl), F32)
    for b in range(PEER_TOPK):
        inner = inner + jnp.where(limit > b, p2s[b:b + 1, :], 0.0)
    inv_z = 1.0 / jnp.sum(p1s * inner, axis=0, keepdims=True)
    lim_e1 = jnp.zeros(rank1.shape, I32)
    for a in range(PEER_TOPK):
        lim_e1 = jnp.where(rank1 == a, limit[a:a + 1, :], lim_e1)
    lh_scr[h] = lim_e1.astype(F32)
    p1_scr[h] = jnp.where(rank1 < PEER_TOPK, jnp.exp(s1 - t1[0:1, :]), 0.0) * inv_z
    r2_scr[h] = rank2.astype(F32).astype(BF16)
    p2_scr[h] = jnp.exp(jnp.minimum(s2 - t2[0:1, :], 0.0)).astype(BF16)


def _peer_body(qp_ref, h2_ref, x1_ref, k1_ref, k2_ref, u_ref, vt_ref, fg_ref, y_ref,
               lh_scr, p1_scr, r2_scr, p2_scr, tv_scr, acc_scr, sca_scr, scb_scr, *, tl, te):
    e = pl.program_id(1)

    @pl.when(e == 0)
    def _():
        acc_scr[...] = jnp.zeros(acc_scr.shape, F32)

        def one_head(h, carry):
            _peer_select(h, qp_ref, k1_ref, k2_ref, tv_scr, lh_scr, p1_scr, r2_scr, p2_scr, tl)
            return carry

        lax.fori_loop(0, PEER_HEADS, one_head, 0)

    e1_per_sub = PEER_SUB // PEER_KEYS
    n_sub = te // PEER_SUB
    sc_bufs = (sca_scr, scb_scr)

    def pre_act(s):
        sc_bufs[s % 2][...] = _nt(u_ref[s * PEER_SUB:(s + 1) * PEER_SUB, :], h2_ref[...])

    pre_act(0)
    for s in range(n_sub):
        if s + 1 < n_sub:
            pre_act(s + 1)
        sc = sc_bufs[s % 2][...]
        act = (0.5 * sc * (1.0 + lax.erf(sc * 0.7071067811865476))).astype(BF16)
        parts = []
        for i in range(e1_per_sub):
            e1 = e * (te // PEER_KEYS) + s * e1_per_sub + i
            coef = jnp.zeros((PEER_KEYS, tl), BF16)
            for h in range(PEER_HEADS):
                lim = lh_scr[h, pl.ds(e1, 1), :].astype(BF16)
                p1 = p1_scr[h, pl.ds(e1, 1), :].astype(BF16)
                coef = coef + jnp.where(r2_scr[h] < lim, p2_scr[h], jnp.zeros((), BF16)) * p1
            parts.append(coef * act[i * PEER_KEYS:(i + 1) * PEER_KEYS, :])
        z = jnp.concatenate(parts, axis=0)
        acc_scr[...] += _mm(vt_ref[:, s * PEER_SUB:(s + 1) * PEER_SUB], z)

    @pl.when(e == pl.num_programs(1) - 1)
    def _():
        y_ref[...] = _rms(x1_ref[...] + acc_scr[...].T, fg_ref[...])


def _peer(qp, h2, x1, wts, tl, te):
    t = h2.shape[0]
    once = pl.Buffered(1)
    keys = lambda: pl.BlockSpec((PEER_KEYS, PEER_HALF), lambda i, e: (0, 0))
    sel = lambda dt=F32: pltpu.VMEM((PEER_HEADS, PEER_KEYS, tl), dt)
    return pl.pallas_call(
        functools.partial(_peer_body, tl=tl, te=te),
        grid=(t // tl, PEER_EXPERTS // te),
        in_specs=[pl.BlockSpec((tl, D_MODEL), lambda i, e: (i, 0), pipeline_mode=once),
                  pl.BlockSpec((tl, D_MODEL), lambda i, e: (i, 0)),
                  pl.BlockSpec((tl, D_MODEL), lambda i, e: (i, 0), pipeline_mode=once),
                  keys(), keys(),
                  pl.BlockSpec((te, D_MODEL), lambda i, e: (e, 0)),
                  pl.BlockSpec((D_MODEL, te), lambda i, e: (0, e)),
                  pl.BlockSpec((1, D_MODEL), lambda i, e: (0, 0))],
        out_specs=pl.BlockSpec((tl, D_MODEL), lambda i, e: (i, 0)),
        out_shape=_sds((t, D_MODEL)),
        scratch_shapes=[sel(), sel(), sel(BF16), sel(BF16), pltpu.VMEM((2, PEER_TOPK, tl), F32),
                        pltpu.VMEM((D_MODEL, tl), F32), pltpu.VMEM((PEER_SUB, tl), F32),
                        pltpu.VMEM((PEER_SUB, tl), F32)],
        compiler_params=_cparams("parallel", "arbitrary"),
        name="peer",
    )(qp, h2, x1, wts["peer_k1"], wts["peer_k2"], wts["peer_u"], wts["peer_vt"], wts["final_norm_g"])


def _pick(n, prefs):
    for p in prefs:
        if n % p == 0:
            return p
    return n


def _group(x3, k_arr, k_cb, v_arr, v_cb, s_gla0, s_gdn0, conv0, wts):
    nb, seq, d = x3.shape
    t = nb * seq
    x2d = x3.reshape(t, d)
    proj, small = _inproj(x2d, wts["norm_mix_g"], wts["w_main"], wts["w_small"],
                          tm=_pick(t, (1024, 512, 256, 128)), tn=1024)
    gla_ch = 16 if seq % 16 == 0 else SUBLANES
    gla_lt = _pick(seq, (256, 128, 64, 32, 16))
    o_a, s_gla = _gla(proj, small, wts, s_gla0, nb, seq, gla_ch, gla_lt)
    o_b, s_gdn, conv_new = _gdn(proj, small, wts, s_gdn0, conv0, nb, seq, _pick(seq, (64, 32, 16)))
    o_c = _xattn(proj, k_arr, k_cb, v_arr, v_cb, nb, seq, _pick(seq, (512, 256, 128)))
    merged = _merge(o_a, o_b, o_c, proj, wts, tm=_pick(t, (256, 128)))
    x1, h2 = _outproj(merged, wts["w_out"], x2d, wts["norm_ffn_g"], tm=_pick(t, (512, 256, 128)))
    qp = _peerq(h2, wts["peer_wq"], tm=_pick(t, (512, 256, 128)))
    y = _peer(qp, h2, x1, wts, tl=_pick(t, (512, 256, 128)), te=1024)
    return y.reshape(nb, seq, d), s_gla, s_gdn, conv_new


def _lanes(vec, start):
    return jnp.zeros((1, SMALL_W), F32).at[0, start:start + vec.shape[0]].set(vec.astype(F32))


def kernel(x_prompt, x_sample, mem_prompt, cache_mem_k, cache_mem_v, state_gla, state_gdn, state_conv, norm_mix_g, norm_mem_g, w_in, w_gla_gate, b_gla_gate, gla_norm_g, gdn_conv_w, gdn_a_log, gdn_dt_bias, gdn_norm_g, w_mem_kv, w_br_gla, w_br_gdn, w_br_mem, b_gates, w_out, norm_ffn_g, peer_wq, peer_k1, peer_k2, peer_u, peer_v, final_norm_g):
    depth = w_in.shape[0]
    assert depth == 1, "the chain below is written for a single layer"
    l = 0
    w_main, w_small = _regroup(w_in.reshape(D_MODEL, IN_COLS), rb=128)
    wts = {
        "norm_mix_g": norm_mix_g[l][None], "w_main": w_main, "w_small": w_small,
        "w_gla_gate": w_gla_gate[l], "b_gla_gate": b_gla_gate[l][None], "gla_norm_g": gla_norm_g[l][None],
        "gdn_conv_w": gdn_conv_w[l], "a_log_lanes": _lanes(gdn_a_log[l], DA_LANE),
        "dt_bias_lanes": _lanes(gdn_dt_bias[l], DA_LANE), "gdn_norm_g": gdn_norm_g[l][None],
        "w_br_gla": w_br_gla[l].astype(BF16), "w_br_gdn": w_br_gdn[l].astype(BF16),
        "w_br_mem": w_br_mem[l].astype(BF16), "b_gates": b_gates[l][None], "w_out": w_out[l].astype(BF16),
        "norm_ffn_g": norm_ffn_g[l][None], "peer_wq": peer_wq[l].astype(BF16),
        "peer_k1": peer_k1[l].astype(BF16), "peer_k2": peer_k2[l].astype(BF16),
        "peer_u": peer_u[l].astype(BF16), "peer_vt": _transpose_cast(peer_v.reshape(PEER_EXPERTS, D_MODEL), tb=512),
        "final_norm_g": final_norm_g[None],
    }
    nb_p, n_mem, _ = mem_prompt.shape
    nb_s = x_sample.shape[0]

    kv = _normmm(mem_prompt.reshape(nb_p * n_mem, D_MODEL), norm_mem_g[l][None], w_mem_kv[l].astype(BF16),
                 tm=_pick(nb_p * n_mem, (512, 256)), tn=1024)
    kv3 = kv.reshape(nb_p, n_mem, 2 * MEM_W)
    y_p, gla_p, gdn_p, conv_p = _group(x_prompt, kv3, 0, kv3, 1, None, None, None, wts)
    mk_p = kv3[:, :, :MEM_W].reshape(nb_p, n_mem, MEM_HEADS, MEM_DH)
    mv_p = kv3[:, :, MEM_W:].reshape(nb_p, n_mem, MEM_HEADS, MEM_DH)

    ck = cache_mem_k.reshape(nb_s, n_mem, MEM_HEADS, MEM_DH)
    cv = cache_mem_v.reshape(nb_s, n_mem, MEM_HEADS, MEM_DH)
    y_s, gla_s, gdn_s, conv_s = _group(x_sample, ck, 0, cv, 0, state_gla[l], state_gdn[l], state_conv[l], wts)

    return (y_p, y_s, gla_p[None], gdn_p[None], conv_p[None], mk_p[None], mv_p[None],
            gla_s[None], gdn_s[None], conv_s[None])
```
